```python
import math
import jax, jax.numpy as jnp
from jax import lax
import numpy as np

D_MODEL = 1024
BATCH = 2
SEQ = 16384
DEPTH = 2

N_META = 16
MLA_HEADS = 8
MLA_Q_RANK = 256
MLA_KV_RANK = 128
MLA_NOPE = 64
MLA_ROPE = 32
MLA_V = 64
ROPE_THETA = 10000.0
ATT_BLOCK = 128
DN_HEADS = 4
DN_DK = 128
DN_DV = 128
DN_CONV = 4
DN_CHUNK = 64
CV_CH = 512
CV_WIDTH = 31
D_FF = 2816
N_EXPERTS = 8
TOP_K = 2
D_FF_EXPERT = 1408
ALPHA = (2 * DEPTH) ** 0.25
BETA = (8 * DEPTH) ** -0.25
EPS = 1e-6
NEG_INF = -1e30

MLA_QK = MLA_NOPE + MLA_ROPE
DN_QKV = 2 * DN_HEADS * DN_DK + DN_HEADS * DN_DV
IN_SIZES = (MLA_Q_RANK, MLA_KV_RANK, MLA_ROPE, DN_QKV, DN_HEADS * DN_DV, DN_HEADS, DN_HEADS, CV_CH, CV_CH, D_MODEL, D_MODEL, D_MODEL)
IN_COLS = sum(IN_SIZES)

kernel_name = 'hybrid_gated_mla_deltanet_conformer_moe'


def _split(t, sizes):
    return jnp.split(t, np.cumsum(sizes)[:-1].tolist(), axis=-1)


def _pad_front(t, n):
    return jnp.pad(t, [(0, 0), (n, 0)] + [(0, 0)] * (t.ndim - 2))


def layer_norm(x, g, b):
    xf = x.astype(jnp.float32)
    mu = jnp.mean(xf, -1, keepdims=True)
    var = jnp.mean(jnp.square(xf - mu), -1, keepdims=True)
    return ((xf - mu) * lax.rsqrt(var + EPS) * g.astype(jnp.float32) + b.astype(jnp.float32)).astype(x.dtype)


def rms_norm(x, g):
    xf = x.astype(jnp.float32)
    return (xf * lax.rsqrt(jnp.mean(jnp.square(xf), -1, keepdims=True) + EPS) * g.astype(jnp.float32)).astype(x.dtype)


def l2_norm(x):
    xf = x.astype(jnp.float32)
    return xf * lax.rsqrt(jnp.sum(jnp.square(xf), -1, keepdims=True) + EPS)


def rope_tables(n_pos, dim):
    inv_freq = ROPE_THETA ** (-jnp.arange(0, dim, 2, dtype=jnp.float32) / dim)
    ang = jnp.arange(n_pos, dtype=jnp.float32)[:, None] * inv_freq[None, :]
    return jnp.cos(ang), jnp.sin(ang)


def apply_rope(x, cos, sin):
    x1, x2 = jnp.split(x.astype(jnp.float32), 2, axis=-1)
    return jnp.concatenate([x1 * cos - x2 * sin, x2 * cos + x1 * sin], -1).astype(x.dtype)


def causal_dwconv(x, w):
    width, ch = w.shape
    return lax.conv_general_dilated(x, w[:, None, :].astype(x.dtype), window_strides=(1,), padding=[(width - 1, 0)], dimension_numbers=('NWC', 'WIO', 'NWC'), feature_group_count=ch)


def mla_branch(c_q, c_kv, k_r, q_norm, w_uq, kv_norm, w_ukv, cos, sin):
    B, L, _ = c_q.shape
    q = (rms_norm(c_q, q_norm) @ w_uq).reshape(B, L, MLA_HEADS, MLA_QK)
    q_nope = q[..., :MLA_NOPE]
    q_rope = apply_rope(q[..., MLA_NOPE:], cos[:, None, :], sin[:, None, :])
    kv = (rms_norm(c_kv, kv_norm) @ w_ukv).reshape(B, L, MLA_HEADS, MLA_NOPE + MLA_V)
    k_nope, v = kv[..., :MLA_NOPE], kv[..., MLA_NOPE:]
    k_rope = apply_rope(k_r, cos, sin)
    pad = ATT_BLOCK - N_META
    q_nope, q_rope, k_nope, k_rope, v = (_pad_front(t, pad) for t in (q_nope, q_rope, k_nope, k_rope, v))
    Lp = L + pad
    n_blocks = Lp // ATT_BLOCK
    scale = MLA_QK ** -0.5
    key_pos = jnp.arange(Lp)

    def one_block(i):
        start = i * ATT_BLOCK
        qn = lax.dynamic_slice_in_dim(q_nope, start, ATT_BLOCK, axis=1)
        qr = lax.dynamic_slice_in_dim(q_rope, start, ATT_BLOCK, axis=1)
        s = jnp.einsum('bqhd,bkhd->bhqk', qn, k_nope) + jnp.einsum('bqhd,bkd->bhqk', qr, k_rope)
        s = s.astype(jnp.float32) * scale
        q_pos = start + jnp.arange(ATT_BLOCK)
        mask = (key_pos[None, :] <= q_pos[:, None]) & (key_pos[None, :] >= pad)
        p = jax.nn.softmax(jnp.where(mask, s, NEG_INF), axis=-1).astype(v.dtype)
        return jnp.einsum('bhqk,bkhd->bqhd', p, v)

    o = lax.map(one_block, jnp.arange(n_blocks))
    o = jnp.transpose(o, (1, 0, 2, 3, 4)).reshape(B, Lp, MLA_HEADS * MLA_V)
    return o[:, pad:]


def gated_deltanet_branch(qkv, z, b_logit, a_logit, conv_w, a_log, dt_bias, o_norm):
    B, L, _ = qkv.shape
    dtype = qkv.dtype
    qkv = jax.nn.silu(causal_dwconv(qkv, conv_w))
    q, k, v = _split(qkv, (DN_HEADS * DN_DK, DN_HEADS * DN_DK, DN_HEADS * DN_DV))
    q = l2_norm(q.reshape(B, L, DN_HEADS, DN_DK)) * (DN_DK ** -0.5)
    k = l2_norm(k.reshape(B, L, DN_HEADS, DN_DK))
    v = v.reshape(B, L, DN_HEADS, DN_DV).astype(jnp.float32)
    beta = jax.nn.sigmoid(b_logit.astype(jnp.float32))
    g = -jnp.exp(a_log.astype(jnp.float32)) * jax.nn.softplus(a_logit.astype(jnp.float32) + dt_bias.astype(jnp.float32))
    pad = DN_CHUNK - N_META
    q, k, v, beta, g = (_pad_front(t, pad) for t in (q, k, v, beta, g))
    Lp = L + pad
    n_chunks = Lp // DN_CHUNK

    def to_chunks(t):
        t = t.reshape((B, n_chunks, DN_CHUNK) + t.shape[2:])
        return jnp.moveaxis(t, 3, 1)

    qc, kc, vc, bc = to_chunks(q), to_chunks(k), to_chunks(v), to_chunks(beta)
    gc = jnp.cumsum(to_chunks(g), axis=-1)
    incl = jnp.tril(jnp.ones((DN_CHUNK, DN_CHUNK), dtype=bool))
    strict = jnp.tril(jnp.ones((DN_CHUNK, DN_CHUNK), dtype=bool), k=-1)
    gamma = jnp.exp(jnp.where(incl, gc[..., :, None] - gc[..., None, :], -jnp.inf))
    kb = kc * bc[..., None]
    a_mat = jnp.where(strict, jnp.einsum('bhnid,bhnjd->bhnij', kb, kc) * gamma, 0.0)
    eye = jnp.eye(DN_CHUNK, dtype=jnp.float32)
    rhs = jnp.concatenate([vc * bc[..., None], kb * jnp.exp(gc)[..., None]], axis=-1)
    sol = lax.linalg.triangular_solve(a_mat + eye, rhs, left_side=True, lower=True, unit_diagonal=True)
    u_base, w_dec = sol[..., :DN_DV], sol[..., DN_DV:]
    qk = jnp.einsum('bhnid,bhnjd->bhnij', qc, kc) * gamma
    q_dec = qc * jnp.exp(gc)[..., None]
    k_dec = kc * jnp.exp(gc[..., -1:] - gc)[..., None]
    chunk_decay = jnp.exp(gc[..., -1])

    def step(state, xs):
        u_b, w_i, qk_i, qd_i, kd_i, cd_i = xs
        u = u_b - jnp.einsum('bhcd,bhde->bhce', w_i, state)
        o = jnp.einsum('bhcd,bhde->bhce', qd_i, state) + jnp.einsum('bhij,bhje->bhie', qk_i, u)
        state = state * cd_i[..., None, None] + jnp.einsum('bhcd,bhce->bhde', kd_i, u)
        return state, o

    xs = tuple(jnp.moveaxis(t, 2, 0) for t in (u_base, w_dec, qk, q_dec, k_dec, chunk_decay))
    s0 = jnp.zeros((B, DN_HEADS, DN_DK, DN_DV), jnp.float32)
    _, o = lax.scan(step, s0, xs)
    o = jnp.transpose(o, (1, 0, 3, 2, 4)).reshape(B, Lp, DN_HEADS, DN_DV)[:, pad:]
    o = rms_norm(o, o_norm) * jax.nn.silu(z.reshape(B, L, DN_HEADS, DN_DV).astype(jnp.float32))
    return o.reshape(B, L, DN_HEADS * DN_DV).astype(dtype)


def conformer_conv_branch(a, b, dw_w, dw_b, ln_g, ln_b):
    h = a * jax.nn.sigmoid(b)
    h = causal_dwconv(h, dw_w) + dw_b
    return jax.nn.silu(layer_norm(h, ln_g, ln_b))


def mixer(h, w_in, q_norm, w_uq, kv_norm, w_ukv, dn_conv_w, dn_a_log, dn_dt_bias, dn_o_norm, cv_dw_w, cv_dw_b, cv_ln_g, cv_ln_b, w_br_mla, w_br_dn, w_br_cv, w_mix_out, cos, sin):
    proj = h @ w_in
    c_q, c_kv, k_r, dn_qkv, dn_z, dn_b, dn_a, cv_a, cv_b, g_mla, g_dn, g_cv = _split(proj, IN_SIZES)
    y_mla = mla_branch(c_q, c_kv, k_r, q_norm, w_uq, kv_norm, w_ukv, cos, sin) @ w_br_mla
    y_dn = gated_deltanet_branch(dn_qkv, dn_z, dn_b, dn_a, dn_conv_w, dn_a_log, dn_dt_bias, dn_o_norm) @ w_br_dn
    y_cv = conformer_conv_branch(cv_a, cv_b, cv_dw_w, cv_dw_b, cv_ln_g, cv_ln_b) @ w_br_cv
    merged = jax.nn.sigmoid(g_mla) * y_mla + jax.nn.sigmoid(g_dn) * y_dn + jax.nn.sigmoid(g_cv) * y_cv
    return merged @ w_mix_out


def swiglu(x, w_gate, w_up, w_down):
    return (jax.nn.silu(x @ w_gate) * (x @ w_up)) @ w_down


def moe_swiglu(x, w_router, w_gate, w_up, w_down):
    B, L, D = x.shape
    xt = x.reshape(B * L, D)
    logits = (xt @ w_router).astype(jnp.float32)
    top_v, top_i = lax.top_k(logits, TOP_K)
    top_w = jax.nn.softmax(top_v, axis=-1)
    gates = jnp.sum(jax.nn.one_hot(top_i, N_EXPERTS, dtype=jnp.float32) * top_w[..., None], axis=1).astype(x.dtype)
    y = jnp.zeros_like(xt)
    for e in range(N_EXPERTS):
        y = y + gates[:, e:e + 1] * swiglu(xt, w_gate[e], w_up[e], w_down[e])
    return y.reshape(B, L, D)


def setup_inputs(seed: int = 0) -> dict:
    key = jax.random.key(seed)
    keys = iter(jax.random.split(key, 48))
    n_dense = (DEPTH + 1) // 2
    n_moe = DEPTH // 2

    def nrm(shape, scale):
        return jax.random.normal(next(keys), shape, jnp.float32) * scale

    dt = jnp.exp(jax.random.uniform(next(keys), (DEPTH, DN_HEADS), jnp.float32, minval=math.log(1e-3), maxval=math.log(1e-1)))
    return {
        'x': nrm((BATCH, SEQ, D_MODEL), 1.0),
        'meta_tokens': nrm((N_META, D_MODEL), 1.0),
        'w_in': nrm((DEPTH, D_MODEL, IN_COLS), D_MODEL ** -0.5),
        'mla_q_norm': 1.0 + nrm((DEPTH, MLA_Q_RANK), 0.02),
        'mla_w_uq': nrm((DEPTH, MLA_Q_RANK, MLA_HEADS * MLA_QK), MLA_Q_RANK ** -0.5),
        'mla_kv_norm': 1.0 + nrm((DEPTH, MLA_KV_RANK), 0.02),
        'mla_w_ukv': nrm((DEPTH, MLA_KV_RANK, MLA_HEADS * (MLA_NOPE + MLA_V)), MLA_KV_RANK ** -0.5),
        'dn_conv_w': nrm((DEPTH, DN_CONV, DN_QKV), DN_CONV ** -0.5),
        'dn_a_log': jnp.log(jax.random.uniform(next(keys), (DEPTH, DN_HEADS), jnp.float32, minval=1.0, maxval=16.0)),
        'dn_dt_bias': dt + jnp.log(-jnp.expm1(-dt)),
        'dn_o_norm': 1.0 + nrm((DEPTH, DN_DV), 0.02),
        'cv_dw_w': nrm((DEPTH, CV_WIDTH, CV_CH), CV_WIDTH ** -0.5),
        'cv_dw_b': nrm((DEPTH, CV_CH), 0.01),
        'cv_ln_g': 1.0 + nrm((DEPTH, CV_CH), 0.02),
        'cv_ln_b': nrm((DEPTH, CV_CH), 0.01),
        'w_br_mla': nrm((DEPTH, MLA_HEADS * MLA_V, D_MODEL), (MLA_HEADS * MLA_V) ** -0.5 * BETA),
        'w_br_dn': nrm((DEPTH, DN_HEADS * DN_DV, D_MODEL), (DN_HEADS * DN_DV) ** -0.5 * BETA),
        'w_br_cv': nrm((DEPTH, CV_CH, D_MODEL), CV_CH ** -0.5 * BETA),
        'w_mix_out': nrm((DEPTH, D_MODEL, D_MODEL), D_MODEL ** -0.5 * BETA),
        'ln1_g': 1.0 + nrm((DEPTH, D_MODEL), 0.02),
        'ln1_b': nrm((DEPTH, D_MODEL), 0.01),
        'ln2_g': 1.0 + nrm((DEPTH, D_MODEL), 0.02),
        'ln2_b': nrm((DEPTH, D_MODEL), 0.01),
        'ffn_w_gate': nrm((n_dense, D_MODEL, D_FF), D_MODEL ** -0.5),
        'ffn_w_up': nrm((n_dense, D_MODEL, D_FF), D_MODEL ** -0.5),
        'ffn_w_down': nrm((n_dense, D_FF, D_MODEL), D_FF ** -0.5 * BETA),
        'moe_w_router': nrm((n_moe, D_MODEL, N_EXPERTS), D_MODEL ** -0.5),
        'moe_w_gate': nrm((n_moe, N_EXPERTS, D_MODEL, D_FF_EXPERT), D_MODEL ** -0.5),
        'moe_w_up': nrm((n_moe, N_EXPERTS, D_MODEL, D_FF_EXPERT), D_MODEL ** -0.5),
        'moe_w_down': nrm((n_moe, N_EXPERTS, D_FF_EXPERT, D_MODEL), D_FF_EXPERT ** -0.5 * BETA),
    }


def reference(x, meta_tokens, w_in, mla_q_norm, mla_w_uq, mla_kv_norm, mla_w_ukv, dn_conv_w, dn_a_log, dn_dt_bias, dn_o_norm, cv_dw_w, cv_dw_b, cv_ln_g, cv_ln_b, w_br_mla, w_br_dn, w_br_cv, w_mix_out, ln1_g, ln1_b, ln2_g, ln2_b, ffn_w_gate, ffn_w_up, ffn_w_down, moe_w_router, moe_w_gate, moe_w_up, moe_w_down):
    B = x.shape[0]
    meta = jnp.broadcast_to(meta_tokens[None].astype(x.dtype), (B, N_META, D_MODEL))
    h = jnp.concatenate([meta, x], axis=1)
    cos, sin = rope_tables(h.shape[1], MLA_ROPE)
    for l in range(DEPTH):
        mix = mixer(h, w_in[l], mla_q_norm[l], mla_w_uq[l], mla_kv_norm[l], mla_w_ukv[l], dn_conv_w[l], dn_a_log[l], dn_dt_bias[l], dn_o_norm[l], cv_dw_w[l], cv_dw_b[l], cv_ln_g[l], cv_ln_b[l], w_br_mla[l], w_br_dn[l], w_br_cv[l], w_mix_out[l], cos, sin)
        h = layer_norm(ALPHA * h + mix, ln1_g[l], ln1_b[l])
        if l % 2 == 0:
            f = swiglu(h, ffn_w_gate[l // 2], ffn_w_up[l // 2], ffn_w_down[l // 2])
        else:
            f = moe_swiglu(h, moe_w_router[l // 2], moe_w_gate[l // 2], moe_w_up[l // 2], moe_w_down[l // 2])
        h = layer_norm(ALPHA * h + f, ln2_g[l], ln2_b[l])
    return h[:, N_META:]
```

```python
import functools

import jax
import jax.numpy as jnp
import numpy as np
from jax import lax
from jax.experimental import pallas as pl
from jax.experimental.pallas import tpu as pltpu

D_MODEL = 1024
DEPTH = 2
N_META = 16
MLA_HEADS = 8
MLA_Q_RANK = 256
MLA_KV_RANK = 128
MLA_NOPE = 64
MLA_ROPE = 32
MLA_V = 64
ROPE_THETA = 10000.0
DN_HEADS = 4
DN_DK = 128
DN_DV = 128
DN_CONV = 4
CV_CH = 512
CV_WIDTH = 31
D_FF = 2816
N_EXPERTS = 8
D_FF_EXPERT = 1408
ALPHA = (2 * DEPTH) ** 0.25
EPS = 1e-6
NEG_INF = -1e30
MLA_QK = MLA_NOPE + MLA_ROPE
DN_QKV = 2 * DN_HEADS * DN_DK + DN_HEADS * DN_DV

LANES = 128
FRONT_PAD = 240
ROW_TILE = 512
DN_CHUNK = 128
CV_HALO = 32
DN_HALO = 8
MISC_B = 64
MISC_A = 68
VMEM_LIMIT = 56 * 1024 * 1024

BF16 = jnp.bfloat16
F32 = jnp.float32


def _dot(a, b):
    return jnp.dot(a, b, preferred_element_type=F32)


def _dot_nt(a, b):
    return lax.dot_general(a, b, (((1,), (1,)), ((), ())), preferred_element_type=F32)


def _dot_tn(a, b):
    return lax.dot_general(a, b, (((0,), (0,)), ((), ())), preferred_element_type=F32)


def _rms(x, g):
    return x * lax.rsqrt(jnp.mean(x * x, axis=-1, keepdims=True) + EPS) * g


def _layer_norm(x, g, b):
    mu = jnp.mean(x, axis=-1, keepdims=True)
    xc = x - mu
    var = jnp.mean(xc * xc, axis=-1, keepdims=True)
    return xc * lax.rsqrt(var + EPS) * g + b


def _silu(x):
    return x * jax.nn.sigmoid(x)


def _const_spec(shape):
    return pl.BlockSpec(shape, lambda *_: (0,) * len(shape))


def _proj_kernel(h_ref, wa_ref, wq_ref, wk_ref, wv_ref, wb_ref, wc_ref, wd_ref, qg_ref, kvg_ref, ct_ref, st_ref, kc_ref,
                 q_ref, k_ref, v_ref, dn_ref, z_ref, glu_ref, misc_ref):
    x = h_ref[...].astype(BF16)
    pa = _dot(x, wa_ref[...])
    cq = pa[:, :MLA_Q_RANK]
    ckv = pa[:, MLA_Q_RANK:MLA_Q_RANK + MLA_KV_RANK]
    blk = pa[:, MLA_Q_RANK + MLA_KV_RANK:]
    cqn = _rms(cq, qg_ref[...])
    qq = _dot(cqn.astype(BF16), wq_ref[...])
    ct = ct_ref[...]
    st = st_ref[...]
    hw = MLA_HEADS * LANES
    for h in range(MLA_HEADS):
        sl = slice(h * LANES, (h + 1) * LANES)
        sp = slice(hw + h * LANES, hw + (h + 1) * LANES)
        q_ref[:, sl] = (qq[:, sl] * ct + qq[:, sp] * st).astype(BF16)
    ckvn = _rms(ckv, kvg_ref[...])
    prod = blk * kc_ref[...]
    krr = prod + pltpu.roll(prod, LANES - MLA_ROPE, axis=1)
    kin = jnp.concatenate([ckvn, krr], axis=1).astype(BF16)
    k_ref[...] = _dot(kin, wk_ref[...]).astype(BF16)
    v_ref[...] = _dot(ckvn.astype(BF16), wv_ref[...]).astype(BF16)
    dn_ref[...] = _dot(x, wb_ref[...])
    z_ref[...] = _dot(x, wc_ref[...])
    cv = _dot(x, wd_ref[...])
    glu_ref[...] = cv[:, :CV_CH] * jax.nn.sigmoid(cv[:, CV_CH:])
    misc_ref[...] = blk


def _proj_call(h, lw, tabs):
    tp = h.shape[0]
    tm = ROW_TILE
    row = lambda w: pl.BlockSpec((tm, w), lambda i: (i, 0))
    ins = [h, lw['wa'], lw['wq'], lw['wk'], lw['wv'], lw['wb'], lw['wc'], lw['wd'], lw['qg'], lw['kvg'],
           tabs['ct'], tabs['st'], tabs['kc']]
    in_specs = [row(D_MODEL)] + [_const_spec(a.shape) for a in ins[1:10]] + [row(LANES)] * 3
    out_shape = [
        jax.ShapeDtypeStruct((tp, MLA_HEADS * LANES), BF16),
        jax.ShapeDtypeStruct((tp, MLA_HEADS * LANES), BF16),
        jax.ShapeDtypeStruct((tp, MLA_HEADS * MLA_V), BF16),
        jax.ShapeDtypeStruct((tp, DN_QKV), F32),
        jax.ShapeDtypeStruct((tp, DN_HEADS * DN_DV), F32),
        jax.ShapeDtypeStruct((tp, CV_CH), F32),
        jax.ShapeDtypeStruct((tp, LANES), F32),
    ]
    out_specs = [row(s.shape[1]) for s in out_shape]
    return pl.pallas_call(
        _proj_kernel,
        grid=(tp // tm,),
        in_specs=in_specs,
        out_specs=out_specs,
        out_shape=out_shape,
        compiler_params=pltpu.CompilerParams(dimension_semantics=("parallel",), vmem_limit_bytes=VMEM_LIMIT),
        name="proj",
    )(*ins)


def _attn_kernel(it_ref, jt_ref, q_ref, k_ref, v_ref, o_ref, m_scr, l_scr, acc_scr, *, tile):
    step = pl.program_id(1)
    i = it_ref[step]
    j = jt_ref[step]

    @pl.when(j == 0)
    def _():
        m_scr[...] = jnp.full(m_scr.shape, NEG_INF, F32)
        l_scr[...] = jnp.zeros(l_scr.shape, F32)
        acc_scr[...] = jnp.zeros(acc_scr.shape, F32)

    def body(masked):
        if masked:
            qpos = i * tile + lax.broadcasted_iota(jnp.int32, (tile, tile), 0)
            kpos = j * tile + lax.broadcasted_iota(jnp.int32, (tile, tile), 1)
            keep = (kpos <= qpos) & (kpos >= FRONT_PAD)
        for h in range(MLA_HEADS):
            sl = slice(h * LANES, (h + 1) * LANES)
            s = _dot_nt(q_ref[:, sl], k_ref[:, sl])
            if masked:
                s = jnp.where(keep, s, NEG_INF)
            m_prev = m_scr[h]
            m_new = jnp.maximum(m_prev, jnp.max(s, axis=1, keepdims=True))
            alpha = jnp.exp(m_prev - m_new)
            p = jnp.exp(s - m_new[:, :1])
            l_scr[h] = alpha * l_scr[h] + jnp.sum(p, axis=1, keepdims=True)
            m_scr[h] = m_new
            vp = v_ref[:, (h // 2) * LANES:(h // 2 + 1) * LANES]
            acc_scr[h] = alpha * acc_scr[h] + _dot(p.astype(BF16), vp)

    edge = (j == i) | (j == 0)
    pl.when(edge)(lambda: body(True))
    pl.when(jnp.logical_not(edge))(lambda: body(False))

    @pl.when(j == i)
    def _():
        lane = lax.broadcasted_iota(jnp.int32, (tile, LANES), 1)
        for hp in range(MLA_HEADS // 2):
            lo = acc_scr[2 * hp] / l_scr[2 * hp]
            hi = acc_scr[2 * hp + 1] / l_scr[2 * hp + 1]
            o_ref[:, hp * LANES:(hp + 1) * LANES] = jnp.where(lane < MLA_V, lo, hi).astype(BF16)


def _attn_call(q, k, v, batch):
    tp = q.shape[0]
    lp = tp // batch
    tile = ROW_TILE
    nb = lp // tile
    pairs = [(i, j) for i in range(nb) for j in range(i + 1)]
    it = jnp.asarray(np.array([p[0] for p in pairs], np.int32))
    jt = jnp.asarray(np.array([p[1] for p in pairs], np.int32))
    qw = MLA_HEADS * LANES
    vw = MLA_HEADS * MLA_V
    grid_spec = pltpu.PrefetchScalarGridSpec(
        num_scalar_prefetch=2,
        grid=(batch, len(pairs)),
        in_specs=[
            pl.BlockSpec((tile, qw), lambda b, s, it, jt: (b * nb + it[s], 0)),
            pl.BlockSpec((tile, qw), lambda b, s, it, jt: (b * nb + jt[s], 0)),
            pl.BlockSpec((tile, vw), lambda b, s, it, jt: (b * nb + jt[s], 0)),
        ],
        out_specs=pl.BlockSpec((tile, vw), lambda b, s, it, jt: (b * nb + it[s], 0)),
        scratch_shapes=[
            pltpu.VMEM((MLA_HEADS, tile, LANES), F32),
            pltpu.VMEM((MLA_HEADS, tile, LANES), F32),
            pltpu.VMEM((MLA_HEADS, tile, LANES), F32),
        ],
    )
    return pl.pallas_call(
        functools.partial(_attn_kernel, tile=tile),
        grid_spec=grid_spec,
        out_shape=jax.ShapeDtypeStruct((tp, vw), BF16),
        compiler_params=pltpu.CompilerParams(dimension_semantics=("parallel", "arbitrary"), vmem_limit_bytes=VMEM_LIMIT),
        name="mla_attn",
    )(it, jt, q, k, v)


def _dn_kernel(x_ref, halo_ref, z_ref, misc_ref, cw_ref, par_ref, on_ref, o_ref, buf, s_scr, *, tile, tiles_per_batch):
    t = pl.program_id(0)
    tb = t % tiles_per_batch
    first = tb == 0

    @pl.when(first)
    def _():
        s_scr[...] = jnp.zeros(s_scr.shape, F32)

    buf[0:DN_HALO, :] = jnp.where(first, 0.0, halo_ref[...])
    buf[DN_HALO:DN_HALO + tile, :] = x_ref[...]
    y = cw_ref[0:1, :] * buf[pl.ds(DN_HALO - DN_CONV + 1, tile), :]
    for jj in range(1, DN_CONV):
        y = y + cw_ref[jj:jj + 1, :] * buf[pl.ds(DN_HALO - DN_CONV + 1 + jj, tile), :]
    y = _silu(y)

    misc = misc_ref[...]
    row = tb * tile + lax.broadcasted_iota(jnp.int32, (tile, LANES), 0)
    valid = row >= FRONT_PAD
    beta_all = jnp.where(valid, jax.nn.sigmoid(misc), 0.0)
    g_all = jnp.where(valid, -jnp.exp(par_ref[0:1, :]) * jax.nn.softplus(misc + par_ref[1:2, :]), 0.0)
    rin = lax.broadcasted_iota(jnp.int32, (tile, LANES), 0) % DN_CHUNK
    gc_all = g_all
    sh = 1
    while sh < DN_CHUNK:
        gc_all = gc_all + jnp.where(rin >= sh, pltpu.roll(gc_all, sh, axis=0), 0.0)
        sh *= 2
    gct_all = gc_all.T

    ii = lax.broadcasted_iota(jnp.int32, (DN_CHUNK, DN_CHUNK), 0)
    jj_ = lax.broadcasted_iota(jnp.int32, (DN_CHUNK, DN_CHUNK), 1)
    incl = ii >= jj_
    strict = ii > jj_
    eye = jnp.where(ii == jj_, 1.0, 0.0).astype(F32)
    onorm = on_ref[...]

    for h in range(DN_HEADS):
        hs = slice(h * DN_DK, (h + 1) * DN_DK)
        qh = y[:, hs]
        kh = y[:, DN_HEADS * DN_DK + h * DN_DK:DN_HEADS * DN_DK + (h + 1) * DN_DK]
        vh = y[:, 2 * DN_HEADS * DN_DK + h * DN_DV:2 * DN_HEADS * DN_DK + (h + 1) * DN_DV]
        qh = qh * lax.rsqrt(jnp.sum(qh * qh, axis=-1, keepdims=True) + EPS) * (DN_DK ** -0.5)
        kh = kh * lax.rsqrt(jnp.sum(kh * kh, axis=-1, keepdims=True) + EPS)
        for c in range(tile // DN_CHUNK):
            rs = slice(c * DN_CHUNK, (c + 1) * DN_CHUNK)
            gcol = gc_all[rs, MISC_A + h:MISC_A + h + 1]
            grow = gct_all[MISC_A + h:MISC_A + h + 1, rs]
            bcol = beta_all[rs, MISC_B + h:MISC_B + h + 1]
            gamma = jnp.where(incl, jnp.exp(jnp.where(incl, gcol - grow, 0.0)), 0.0)
            qc = qh[rs]
            kc = kh[rs]
            vc = vh[rs]
            kb = kc * bcol
            a_mat = jnp.where(strict, _dot_nt(kb, kc) * gamma, 0.0)
            eg = jnp.exp(gcol)
            rhs = jnp.concatenate([vc * bcol, kb * eg], axis=1)
            xp = -a_mat
            tinv = eye + xp
            for _ in range(6):
                xp = _dot(xp, xp)
                tinv = tinv + _dot(tinv, xp)
            sol = _dot(tinv, rhs)
            u_base = sol[:, :DN_DV]
            w_dec = sol[:, DN_DV:]
            qk = jnp.where(incl, _dot_nt(qc, kc) * gamma, 0.0)
            glast = gcol[DN_CHUNK - 1:DN_CHUNK, :]
            q_dec = qc * eg
            k_dec = kc * jnp.exp(glast - gcol)
            state = s_scr[h]
            u = u_base - _dot(w_dec, state)
            o = _dot(q_dec, state) + _dot(qk, u)
            s_scr[h] = state * jnp.exp(glast) + _dot_tn(k_dec, u)
            on = _rms(o, onorm)
            o_ref[rs, h * DN_DV:(h + 1) * DN_DV] = (on * _silu(z_ref[rs, h * DN_DV:(h + 1) * DN_DV])).astype(BF16)


def _dn_call(dn_pre, z, misc, lw, batch):
    tp = dn_pre.shape[0]
    tile = ROW_TILE
    tpb = tp // batch // tile
    hb = tile // DN_HALO
    return pl.pallas_call(
        functools.partial(_dn_kernel, tile=tile, tiles_per_batch=tpb),
        grid=(tp // tile,),
        in_specs=[
            pl.BlockSpec((tile, DN_QKV), lambda t: (t, 0)),
            pl.BlockSpec((DN_HALO, DN_QKV), lambda t: (jnp.maximum(t * hb - 1, 0), 0)),
            pl.BlockSpec((tile, DN_HEADS * DN_DV), lambda t: (t, 0)),
            pl.BlockSpec((tile, LANES), lambda t: (t, 0)),
            _const_spec(lw['dn_cw'].shape),
            _const_spec(lw['dn_par'].shape),
            _const_spec(lw['dn_on'].shape),
        ],
        out_specs=pl.BlockSpec((tile, DN_HEADS * DN_DV), lambda t: (t, 0)),
        out_shape=jax.ShapeDtypeStruct((tp, DN_HEADS * DN_DV), BF16),
        scratch_shapes=[
            pltpu.VMEM((DN_HALO + tile, DN_QKV), F32),
            pltpu.VMEM((DN_HEADS, DN_DK, DN_DV), F32),
        ],
        compiler_params=pltpu.CompilerParams(dimension_semantics=("arbitrary",), vmem_limit_bytes=VMEM_LIMIT),
        name="deltanet",
    )(dn_pre, dn_pre, z, misc, lw['dn_cw'], lw['dn_par'], lw['dn_on'])


def _cv_kernel(x_ref, halo_ref, w_ref, b_ref, g_ref, be_ref, o_ref, buf, *, tile, tiles_per_batch):
    t = pl.program_id(0)
    first = (t % tiles_per_batch) == 0
    buf[0:CV_HALO, :] = jnp.where(first, 0.0, halo_ref[...])
    buf[CV_HALO:CV_HALO + tile, :] = x_ref[...]
    base = CV_HALO - CV_WIDTH + 1
    acc = w_ref[0:1, :] * buf[pl.ds(base, tile), :]
    for j in range(1, CV_WIDTH):
        acc = acc + w_ref[j:j + 1, :] * buf[pl.ds(base + j, tile), :]
    acc = acc + b_ref[...]
    o_ref[...] = _silu(_layer_norm(acc, g_ref[...], be_ref[...])).astype(BF16)


def _cv_call(glu, lw, batch):
    tp = glu.shape[0]
    tile = ROW_TILE
    tpb = tp // batch // tile
    hb = tile // CV_HALO
    return pl.pallas_call(
        functools.partial(_cv_kernel, tile=tile, tiles_per_batch=tpb),
        grid=(tp // tile,),
        in_specs=[
            pl.BlockSpec((tile, CV_CH), lambda t: (t, 0)),
            pl.BlockSpec((CV_HALO, CV_CH), lambda t: (jnp.maximum(t * hb - 1, 0), 0)),
            _const_spec(lw['cv_w'].shape),
            _const_spec(lw['cv_b'].shape),
            _const_spec(lw['cv_g'].shape),
            _const_spec(lw['cv_be'].shape),
        ],
        out_specs=pl.BlockSpec((tile, CV_CH), lambda t: (t, 0)),
        out_shape=jax.ShapeDtypeStruct((tp, CV_CH), BF16),
        scratch_shapes=[pltpu.VMEM((CV_HALO + tile, CV_CH), F32)],
        compiler_params=pltpu.CompilerParams(dimension_semantics=("parallel",), vmem_limit_bytes=VMEM_LIMIT),
        name="cv_conv",
    )(glu, glu, lw['cv_w'], lw['cv_b'], lw['cv_g'], lw['cv_be'])


def _merge_kernel(h_ref, om_ref, od_ref, oc_ref, wg_ref, wbm_ref, wbd_ref, wbc_ref, wmo_ref, g_ref, b_ref, wr_ref,
                  o_ref, gate_ref, *, tile, tiles_per_batch, seq_rows, route):
    hf = h_ref[...]
    x = hf.astype(BF16)
    merged = None
    for br, (src, wbr) in enumerate(((om_ref, wbm_ref), (od_ref, wbd_ref), (oc_ref, wbc_ref))):
        gl = _dot(x, wg_ref[:, br * D_MODEL:(br + 1) * D_MODEL])
        yb = _dot(src[...], wbr[...])
        term = jax.nn.sigmoid(gl) * yb
        merged = term if merged is None else merged + term
    mix = _dot(merged.astype(BF16), wmo_ref[...])
    h1 = _layer_norm(ALPHA * hf + mix, g_ref[...], b_ref[...])
    row = (pl.program_id(0) % tiles_per_batch) * tile + lax.broadcasted_iota(jnp.int32, (tile, 1), 0)
    live = (row >= FRONT_PAD) & (row < seq_rows)
    h1 = jnp.where(live, h1, 0.0)
    o_ref[...] = h1
    if route:
        logits = jnp.dot(h1, wr_ref[...], preferred_element_type=F32, precision=lax.Precision.HIGHEST)
        lane = lax.broadcasted_iota(jnp.int32, (tile, LANES), 1)
        logits = jnp.where(lane < N_EXPERTS, logits, -jnp.inf)
        m1 = jnp.max(logits, axis=1, keepdims=True)
        i1 = jnp.min(jnp.where(logits == m1, lane, LANES), axis=1, keepdims=True)
        rest = jnp.where(lane == i1, -jnp.inf, logits)
        m2 = jnp.max(rest, axis=1, keepdims=True)
        i2 = jnp.min(jnp.where(rest == m2, lane, LANES), axis=1, keepdims=True)
        e2 = jnp.exp(m2 - m1)
        den = 1.0 + e2
        gate_ref[...] = jnp.where(lane == i1, 1.0 / den, jnp.where(lane == i2, e2 / den, 0.0))
    else:
        gate_ref[...] = jnp.ones(gate_ref.shape, F32)


def _merge_call(h, o_mla, o_dn, o_cv, lw, batch, seq_rows, route):
    tp = h.shape[0]
    tile = ROW_TILE
    tpb = tp // batch // tile
    row = lambda w: pl.BlockSpec((tile, w), lambda i: (i, 0))
    consts = [lw['wg'], lw['wbm'], lw['wbd'], lw['wbc'], lw['wmo'], lw['ln1_g'], lw['ln1_b'], lw['wr']]
    return pl.pallas_call(
        functools.partial(_merge_kernel, tile=tile, tiles_per_batch=tpb, seq_rows=seq_rows, route=route),
        grid=(tp // tile,),
        in_specs=[row(D_MODEL), row(512), row(512), row(512)] + [_const_spec(a.shape) for a in consts],
        out_specs=[row(D_MODEL), row(LANES)],
        out_shape=[jax.ShapeDtypeStruct((tp, D_MODEL), F32), jax.ShapeDtypeStruct((tp, LANES), F32)],
        compiler_params=pltpu.CompilerParams(dimension_semantics=("parallel",), vmem_limit_bytes=VMEM_LIMIT),
        name="merge_ln1",
    )(h, o_mla, o_dn, o_cv, *consts)


def _ffn_kernel(h_ref, gate_ref, wg_ref, wu_ref, wd_ref, g_ref, b_ref, o_ref, acc, *, tile, tiles_per_batch, seq_rows, gated):
    c = pl.program_id(1)
    x = h_ref[...].astype(BF16)
    hid = _silu(_dot(x, wg_ref[0])) * _dot(x, wu_ref[0])
    y = _dot(hid.astype(BF16), wd_ref[0])
    if gated:
        lane = lax.broadcasted_iota(jnp.int32, (tile, LANES), 1)
        y = y * jnp.sum(jnp.where(lane == c, gate_ref[...], 0.0), axis=1, keepdims=True)

    @pl.when(c == 0)
    def _():
        acc[...] = y

    @pl.when(c > 0)
    def _():
        acc[...] = acc[...] + y

    @pl.when(c == pl.num_programs(1) - 1)
    def _():
        h2 = _layer_norm(ALPHA * h_ref[...] + acc[...], g_ref[...], b_ref[...])
        row = (pl.program_id(0) % tiles_per_batch) * tile + lax.broadcasted_iota(jnp.int32, (tile, 1), 0)
        live = (row >= FRONT_PAD) & (row < seq_rows)
        o_ref[...] = jnp.where(live, h2, 0.0)


def _ffn_call(h, gates, wg, wu, wd, ln_g, ln_b, batch, seq_rows, gated):
    tp = h.shape[0]
    tile = ROW_TILE
    tpb = tp // batch // tile
    nchunk, _, width = wg.shape
    return pl.pallas_call(
        functools.partial(_ffn_kernel, tile=tile, tiles_per_batch=tpb, seq_rows=seq_rows, gated=gated),
        grid=(tp // tile, nchunk),
        in_specs=[
            pl.BlockSpec((tile, D_MODEL), lambda i, c: (i, 0)),
            pl.BlockSpec((tile, LANES), lambda i, c: (i, 0)),
            pl.BlockSpec((1, D_MODEL, width), lambda i, c: (c, 0, 0)),
            pl.BlockSpec((1, D_MODEL, width), lambda i, c: (c, 0, 0)),
            pl.BlockSpec((1, width, D_MODEL), lambda i, c: (c, 0, 0)),
            _const_spec(ln_g.shape),
            _const_spec(ln_b.shape),
        ],
        out_specs=pl.BlockSpec((tile, D_MODEL), lambda i, c: (i, 0)),
        out_shape=jax.ShapeDtypeStruct((tp, D_MODEL), F32),
        scratch_shapes=[pltpu.VMEM((tile, D_MODEL), F32)],
        compiler_params=pltpu.CompilerParams(dimension_semantics=("parallel", "arbitrary"), vmem_limit_bytes=VMEM_LIMIT),
        name="ffn_ln2",
    )(h, gates, wg, wu, wd, ln_g, ln_b)


def _row(v, width=None):
    v = v.astype(F32).reshape(1, -1)
    if width is not None and v.shape[1] < width:
        v = jnp.pad(v, ((0, 0), (0, width - v.shape[1])))
    return v


def _layer_weights(l, w_in, mla_q_norm, mla_w_uq, mla_kv_norm, mla_w_ukv, dn_conv_w, dn_a_log, dn_dt_bias, dn_o_norm,
                   cv_dw_w, cv_dw_b, cv_ln_g, cv_ln_b, w_br_mla, w_br_dn, w_br_cv, w_mix_out, ln1_g, ln1_b, ln2_g, ln2_b,
                   moe_w_router):
    sizes = (MLA_Q_RANK, MLA_KV_RANK, MLA_ROPE, DN_QKV, DN_HEADS * DN_DV, DN_HEADS, DN_HEADS, CV_CH, CV_CH,
             D_MODEL, D_MODEL, D_MODEL)
    offs = np.cumsum((0,) + sizes)
    col = lambda i: w_in[l][:, offs[i]:offs[i + 1]]
    half = MLA_ROPE // 2
    w_kr = col(2)
    w_kr_pair = jnp.concatenate([-w_kr[:, half:], w_kr[:, :half]], axis=1)
    wa = jnp.concatenate([col(0), col(1), w_kr, w_kr_pair, col(5), col(6)], axis=1)
    wa = jnp.pad(wa, ((0, 0), (0, 512 - wa.shape[1])))
    wuq = mla_w_uq[l].reshape(MLA_Q_RANK, MLA_HEADS, MLA_QK)
    zq = jnp.zeros((MLA_Q_RANK, MLA_HEADS, LANES - MLA_QK), F32)
    wq_main = jnp.concatenate([wuq, zq], axis=2)
    wq_pair = jnp.concatenate([jnp.zeros((MLA_Q_RANK, MLA_HEADS, MLA_NOPE), F32), -wuq[:, :, MLA_NOPE + half:],
                               wuq[:, :, MLA_NOPE:MLA_NOPE + half], zq], axis=2)
    wq = jnp.concatenate([wq_main.reshape(MLA_Q_RANK, -1), wq_pair.reshape(MLA_Q_RANK, -1)], axis=1)
    wukv = mla_w_ukv[l].reshape(MLA_KV_RANK, MLA_HEADS, MLA_NOPE + MLA_V)
    wk_lat = jnp.concatenate([wukv[:, :, :MLA_NOPE], jnp.zeros((MLA_KV_RANK, MLA_HEADS, LANES - MLA_NOPE), F32)], axis=2)
    place = np.zeros((LANES, MLA_HEADS, LANES), np.float32)
    for r in range(MLA_ROPE):
        place[r, :, MLA_NOPE + r] = 1.0
    wk = jnp.concatenate([wk_lat.reshape(MLA_KV_RANK, -1), jnp.asarray(place).reshape(LANES, -1)], axis=0)
    wv = wukv[:, :, MLA_NOPE:].reshape(MLA_KV_RANK, -1)
    par = jnp.zeros((8, LANES), F32)
    par = par.at[0, MISC_A:MISC_A + DN_HEADS].set(dn_a_log[l].astype(F32))
    par = par.at[1, MISC_A:MISC_A + DN_HEADS].set(dn_dt_bias[l].astype(F32))
    if moe_w_router is None:
        wr = jnp.zeros((D_MODEL, LANES), F32)
    else:
        wr = jnp.pad(moe_w_router.astype(F32), ((0, 0), (0, LANES - N_EXPERTS)))
    return dict(
        wa=wa.astype(BF16), wq=wq.astype(BF16), wk=wk.astype(BF16), wv=wv.astype(BF16),
        wb=col(3).astype(BF16), wc=col(4).astype(BF16), wd=jnp.concatenate([col(7), col(8)], axis=1).astype(BF16),
        qg=_row(mla_q_norm[l]), kvg=_row(mla_kv_norm[l]),
        dn_cw=jnp.pad(dn_conv_w[l].astype(F32), ((0, 8 - DN_CONV), (0, 0))), dn_par=par, dn_on=_row(dn_o_norm[l]),
        cv_w=jnp.pad(cv_dw_w[l].astype(F32), ((0, 32 - CV_WIDTH), (0, 0))), cv_b=_row(cv_dw_b[l]),
        cv_g=_row(cv_ln_g[l]), cv_be=_row(cv_ln_b[l]),
        wg=jnp.concatenate([col(9), col(10), col(11)], axis=1).astype(BF16),
        wbm=w_br_mla[l].astype(BF16), wbd=w_br_dn[l].astype(BF16), wbc=w_br_cv[l].astype(BF16),
        wmo=w_mix_out[l].astype(BF16), ln1_g=_row(ln1_g[l]), ln1_b=_row(ln1_b[l]),
        ln2_g=_row(ln2_g[l]), ln2_b=_row(ln2_b[l]), wr=wr,
    )


def _rope_tables(lp):
    pos = jnp.maximum(jnp.arange(lp, dtype=jnp.int32) - FRONT_PAD, 0).astype(F32)
    inv_freq = ROPE_THETA ** (-jnp.arange(0, MLA_ROPE, 2, dtype=F32) / MLA_ROPE)
    ang = pos[:, None] * inv_freq[None, :]
    cos2 = jnp.concatenate([jnp.cos(ang), jnp.cos(ang)], axis=1)
    sin2 = jnp.concatenate([jnp.sin(ang), jnp.sin(ang)], axis=1)
    scale = MLA_QK ** -0.5
    zpad = jnp.zeros((lp, LANES - MLA_QK), F32)
    ct = jnp.concatenate([jnp.full((lp, MLA_NOPE), scale, F32), scale * cos2, zpad], axis=1)
    st = jnp.concatenate([jnp.zeros((lp, MLA_NOPE), F32), scale * sin2, zpad], axis=1)
    kc = jnp.concatenate([cos2, sin2, jnp.zeros((lp, LANES - 2 * MLA_ROPE), F32)], axis=1)
    return ct, st, kc


def kernel(x, meta_tokens, w_in, mla_q_norm, mla_w_uq, mla_kv_norm, mla_w_ukv, dn_conv_w, dn_a_log, dn_dt_bias, dn_o_norm, cv_dw_w, cv_dw_b, cv_ln_g, cv_ln_b, w_br_mla, w_br_dn, w_br_cv, w_mix_out, ln1_g, ln1_b, ln2_g, ln2_b, ffn_w_gate, ffn_w_up, ffn_w_down, moe_w_router, moe_w_gate, moe_w_up, moe_w_down):
    batch, seq, _ = x.shape
    seq_rows = FRONT_PAD + N_META + seq
    lp = -(-seq_rows // ROW_TILE) * ROW_TILE
    meta = jnp.broadcast_to(meta_tokens[None].astype(x.dtype), (batch, N_META, D_MODEL))
    hp = jnp.concatenate([jnp.zeros((batch, FRONT_PAD, D_MODEL), x.dtype), meta, x,
                          jnp.zeros((batch, lp - seq_rows, D_MODEL), x.dtype)], axis=1)
    h = hp.reshape(batch * lp, D_MODEL)
    ct, st, kc = _rope_tables(lp)
    tabs = dict(ct=jnp.tile(ct, (batch, 1)), st=jnp.tile(st, (batch, 1)), kc=jnp.tile(kc, (batch, 1)))
    for l in range(DEPTH):
        moe = l % 2 == 1
        lw = _layer_weights(l, w_in, mla_q_norm, mla_w_uq, mla_kv_norm, mla_w_ukv, dn_conv_w, dn_a_log, dn_dt_bias,
                            dn_o_norm, cv_dw_w, cv_dw_b, cv_ln_g, cv_ln_b, w_br_mla, w_br_dn, w_br_cv, w_mix_out,
                            ln1_g, ln1_b, ln2_g, ln2_b, moe_w_router[l // 2] if moe else None)
        q, k, v, dn_pre, z, glu, misc = _proj_call(h, lw, tabs)
        o_mla = _attn_call(q, k, v, batch)
        o_dn = _dn_call(dn_pre, z, misc, lw, batch)
        o_cv = _cv_call(glu, lw, batch)
        h1, gates = _merge_call(h, o_mla, o_dn, o_cv, lw, batch, seq_rows, moe)
        if moe:
            wg, wu, wd = moe_w_gate[l // 2], moe_w_up[l // 2], moe_w_down[l // 2]
        else:
            nck = D_FF // D_FF_EXPERT
            wg = ffn_w_gate[l // 2].reshape(D_MODEL, nck, D_FF_EXPERT).transpose(1, 0, 2)
            wu = ffn_w_up[l // 2].reshape(D_MODEL, nck, D_FF_EXPERT).transpose(1, 0, 2)
            wd = ffn_w_down[l // 2].reshape(nck, D_FF_EXPERT, D_MODEL)
        h = _ffn_call(h1, gates, wg.astype(BF16), wu.astype(BF16), wd.astype(BF16), lw['ln2_g'], lw['ln2_b'],
                      batch, seq_rows, moe)
    out = h.reshape(batch, lp, D_MODEL)
    return out[:, FRONT_PAD + N_META:FRONT_PAD + N_META + seq]
```

```python
import functools

import jax
import jax.numpy as jnp
import numpy as np
from jax import lax
from jax.experimental import pallas as pl
from jax.experimental.pallas import tpu as pltpu

D_MODEL = 1024
DEPTH = 2
N_META = 16
MLA_HEADS = 8
MLA_Q_RANK = 256
MLA_KV_RANK = 128
MLA_NOPE = 64
MLA_ROPE = 32
MLA_V = 64
ROPE_THETA = 10000.0
DN_HEADS = 4
DN_DK = 128
DN_DV = 128
DN_CONV = 4
CV_CH = 512
CV_WIDTH = 31
D_FF = 2816
N_EXPERTS = 8
D_FF_EXPERT = 1408
ALPHA = (2 * DEPTH) ** 0.25
EPS = 1e-6
NEG_INF = -1e30
MLA_QK = MLA_NOPE + MLA_ROPE
DN_QKV = 2 * DN_HEADS * DN_DK + DN_HEADS * DN_DV

LANES = 128
FRONT_PAD = 240
ROW_TILE = 512
DN_CHUNK = 128
CV_HALO = 32
DN_HALO = 8
MISC_B = 64
MISC_A = 68
VMEM_LIMIT = 56 * 1024 * 1024

BF16 = jnp.bfloat16
F32 = jnp.float32


def _dot(a, b):
    return jnp.dot(a, b, preferred_element_type=F32)


def _dot_nt(a, b):
    return lax.dot_general(a, b, (((1,), (1,)), ((), ())), preferred_element_type=F32)


def _dot_tn(a, b):
    return lax.dot_general(a, b, (((0,), (0,)), ((), ())), preferred_element_type=F32)


def _rms(x, g):
    return x * lax.rsqrt(jnp.mean(x * x, axis=-1, keepdims=True) + EPS) * g


def _layer_norm(x, g, b):
    mu = jnp.mean(x, axis=-1, keepdims=True)
    xc = x - mu
    var = jnp.mean(xc * xc, axis=-1, keepdims=True)
    return xc * lax.rsqrt(var + EPS) * g + b


def _silu(x):
    return x * jax.nn.sigmoid(x)


def _const_spec(shape):
    return pl.BlockSpec(shape, lambda *_: (0,) * len(shape))


def _proj_kernel(h_ref, wa_ref, wq_ref, wk_ref, wv_ref, wb_ref, wc_ref, wd_ref, qg_ref, kvg_ref, vone_ref, ct_ref, st_ref,
                 kc_ref, q_ref, k_ref, v_ref, dn_ref, z_ref, glu_ref, misc_ref):
    x = h_ref[...].astype(BF16)
    pa = _dot(x, wa_ref[...])
    cq = pa[:, :MLA_Q_RANK]
    ckv = pa[:, MLA_Q_RANK:MLA_Q_RANK + MLA_KV_RANK]
    blk = pa[:, MLA_Q_RANK + MLA_KV_RANK:]
    cqn = _rms(cq, qg_ref[...])
    qq = _dot(cqn.astype(BF16), wq_ref[...])
    ct = ct_ref[...]
    st = st_ref[...]
    hw = MLA_HEADS * LANES
    for h in range(MLA_HEADS):
        sl = slice(h * LANES, (h + 1) * LANES)
        sp = slice(hw + h * LANES, hw + (h + 1) * LANES)
        q_ref[:, sl] = (qq[:, sl] * ct + qq[:, sp] * st).astype(BF16)
    ckvn = _rms(ckv, kvg_ref[...])
    prod = blk * kc_ref[...]
    krr = prod + pltpu.roll(prod, LANES - MLA_ROPE, axis=1)
    kin = jnp.concatenate([ckvn, krr], axis=1).astype(BF16)
    k_ref[...] = _dot(kin, wk_ref[...]).astype(BF16)
    v_ref[...] = (_dot(ckvn.astype(BF16), wv_ref[...]) + vone_ref[...]).astype(BF16)
    dn_ref[...] = _dot(x, wb_ref[...])
    z_ref[...] = _dot(x, wc_ref[...])
    cv = _dot(x, wd_ref[...])
    glu_ref[...] = cv[:, :CV_CH] * jax.nn.sigmoid(cv[:, CV_CH:])
    misc_ref[...] = blk


def _proj_call(h, lw, tabs):
    tp = h.shape[0]
    tm = ROW_TILE
    row = lambda w: pl.BlockSpec((tm, w), lambda i: (i, 0))
    ins = [h, lw['wa'], lw['wq'], lw['wk'], lw['wv'], lw['wb'], lw['wc'], lw['wd'], lw['qg'], lw['kvg'], lw['vone'],
           tabs['ct'], tabs['st'], tabs['kc']]
    in_specs = [row(D_MODEL)] + [_const_spec(a.shape) for a in ins[1:11]] + [row(LANES)] * 3
    out_shape = [
        jax.ShapeDtypeStruct((tp, MLA_HEADS * LANES), BF16),
        jax.ShapeDtypeStruct((tp, MLA_HEADS * LANES), BF16),
        jax.ShapeDtypeStruct((tp, MLA_HEADS * LANES), BF16),
        jax.ShapeDtypeStruct((tp, DN_QKV), F32),
        jax.ShapeDtypeStruct((tp, DN_HEADS * DN_DV), F32),
        jax.ShapeDtypeStruct((tp, CV_CH), F32),
        jax.ShapeDtypeStruct((tp, LANES), F32),
    ]
    out_specs = [row(s.shape[1]) for s in out_shape]
    return pl.pallas_call(
        _proj_kernel,
        grid=(tp // tm,),
        in_specs=in_specs,
        out_specs=out_specs,
        out_shape=out_shape,
        compiler_params=pltpu.CompilerParams(dimension_semantics=("parallel",), vmem_limit_bytes=VMEM_LIMIT),
        name="proj",
    )(*ins)


def _attn_kernel(it_ref, jt_ref, q_ref, ka_ref, va_ref, kb_ref, vb_ref, o_ref, m_scr, acc_scr, *, tile):
    step = pl.program_id(1)
    i = it_ref[step]
    ja = jt_ref[step]

    @pl.when(ja == 0)
    def _():
        m_scr[...] = jnp.full(m_scr.shape, NEG_INF, F32)
        acc_scr[...] = jnp.zeros(acc_scr.shape, F32)

    def body(tiles):
        keeps = []
        for _, _, jb, masked in tiles:
            if masked:
                qpos = i * tile + lax.broadcasted_iota(jnp.int32, (tile, tile), 0)
                kpos = jb * tile + lax.broadcasted_iota(jnp.int32, (tile, tile), 1)
                keeps.append((kpos <= qpos) & (kpos >= FRONT_PAD))
            else:
                keeps.append(None)
        for h in range(MLA_HEADS):
            sl = slice(h * LANES, (h + 1) * LANES)
            q = q_ref[:, sl]
            ss = []
            for (kr, _, _, _), keep in zip(tiles, keeps):
                s = _dot_nt(q, kr[:, sl])
                ss.append(s if keep is None else jnp.where(keep, s, NEG_INF))
            m_prev = m_scr[h]
            m_cur = jnp.max(ss[0], axis=1, keepdims=True)
            for s in ss[1:]:
                m_cur = jnp.maximum(m_cur, jnp.max(s, axis=1, keepdims=True))
            m_new = jnp.maximum(m_prev, m_cur)
            m_scr[h] = m_new
            pv = None
            for (_, vr, _, _), s in zip(tiles, ss):
                t = _dot(jnp.exp2(s - m_new[:, :1]).astype(BF16), vr[:, sl])
                pv = t if pv is None else pv + t
            acc_scr[h] = jnp.exp2(m_prev - m_new) * acc_scr[h] + pv

    two = ja + 1 <= i
    edge = (ja == 0) | (ja + 1 >= i)
    pl.when(two & jnp.logical_not(edge))(lambda: body([(ka_ref, va_ref, ja, False), (kb_ref, vb_ref, ja + 1, False)]))
    pl.when(two & edge)(lambda: body([(ka_ref, va_ref, ja, True), (kb_ref, vb_ref, ja + 1, True)]))
    pl.when(jnp.logical_not(two))(lambda: body([(ka_ref, va_ref, ja, True)]))

    @pl.when(ja + 1 >= i)
    def _():
        lane = lax.broadcasted_iota(jnp.int32, (tile, LANES), 1)
        for hp in range(MLA_HEADS // 2):
            a0 = acc_scr[2 * hp]
            a1 = acc_scr[2 * hp + 1]
            lo = a0 / a0[:, MLA_V:MLA_V + 1]
            hi = pltpu.roll(a1 / a1[:, MLA_V:MLA_V + 1], MLA_V, axis=1)
            o_ref[:, hp * LANES:(hp + 1) * LANES] = jnp.where(lane < MLA_V, lo, hi).astype(BF16)


def _attn_call(q, k, v, batch):
    tp = q.shape[0]
    lp = tp // batch
    tile = ROW_TILE
    nb = lp // tile
    steps = [(i, ja) for i in range(nb) for ja in range(0, i + 1, 2)]
    it = jnp.asarray(np.array([p[0] for p in steps], np.int32))
    jt = jnp.asarray(np.array([p[1] for p in steps], np.int32))
    qw = MLA_HEADS * LANES
    first = lambda b, s, it, jt: (b * nb + jt[s], 0)
    second = lambda b, s, it, jt: (b * nb + jnp.minimum(jt[s] + 1, it[s]), 0)
    grid_spec = pltpu.PrefetchScalarGridSpec(
        num_scalar_prefetch=2,
        grid=(batch, len(steps)),
        in_specs=[
            pl.BlockSpec((tile, qw), lambda b, s, it, jt: (b * nb + it[s], 0)),
            pl.BlockSpec((tile, qw), first),
            pl.BlockSpec((tile, qw), first),
            pl.BlockSpec((tile, qw), second),
            pl.BlockSpec((tile, qw), second),
        ],
        out_specs=pl.BlockSpec((tile, MLA_HEADS * MLA_V), lambda b, s, it, jt: (b * nb + it[s], 0)),
        scratch_shapes=[
            pltpu.VMEM((MLA_HEADS, tile, LANES), F32),
            pltpu.VMEM((MLA_HEADS, tile, LANES), F32),
        ],
    )
    return pl.pallas_call(
        functools.partial(_attn_kernel, tile=tile),
        grid_spec=grid_spec,
        out_shape=jax.ShapeDtypeStruct((tp, MLA_HEADS * MLA_V), BF16),
        compiler_params=pltpu.CompilerParams(dimension_semantics=("parallel", "arbitrary"), vmem_limit_bytes=VMEM_LIMIT),
        name="mla_attn",
    )(it, jt, q, k, v, k, v)


def _dn_kernel(x_ref, halo_ref, z_ref, misc_ref, cw_ref, par_ref, on_ref, o_ref, buf, s_scr, *, tile, tiles_per_batch):
    t = pl.program_id(0)
    tb = t % tiles_per_batch
    first = tb == 0

    @pl.when(first)
    def _():
        s_scr[...] = jnp.zeros(s_scr.shape, F32)

    buf[0:DN_HALO, :] = jnp.where(first, 0.0, halo_ref[...])
    buf[DN_HALO:DN_HALO + tile, :] = x_ref[...]
    y = cw_ref[0:1, :] * buf[pl.ds(DN_HALO - DN_CONV + 1, tile), :]
    for jj in range(1, DN_CONV):
        y = y + cw_ref[jj:jj + 1, :] * buf[pl.ds(DN_HALO - DN_CONV + 1 + jj, tile), :]
    y = _silu(y)

    misc = misc_ref[...]
    row = tb * tile + lax.broadcasted_iota(jnp.int32, (tile, LANES), 0)
    valid = row >= FRONT_PAD
    beta_all = jnp.where(valid, jax.nn.sigmoid(misc), 0.0)
    g_all = jnp.where(valid, -jnp.exp(par_ref[0:1, :]) * jax.nn.softplus(misc + par_ref[1:2, :]), 0.0)
    rin = lax.broadcasted_iota(jnp.int32, (tile, LANES), 0) % DN_CHUNK
    gc_all = g_all
    sh = 1
    while sh < DN_CHUNK:
        gc_all = gc_all + jnp.where(rin >= sh, pltpu.roll(gc_all, sh, axis=0), 0.0)
        sh *= 2
    gct_all = gc_all.T

    ii = lax.broadcasted_iota(jnp.int32, (DN_CHUNK, DN_CHUNK), 0)
    jj_ = lax.broadcasted_iota(jnp.int32, (DN_CHUNK, DN_CHUNK), 1)
    incl = ii >= jj_
    strict = ii > jj_
    eye = jnp.where(ii == jj_, 1.0, 0.0).astype(F32)
    onorm = on_ref[...]

    nch = tile // DN_CHUNK
    items = []
    for h in range(DN_HEADS):
        hs = slice(h * DN_DK, (h + 1) * DN_DK)
        qh = y[:, hs]
        kh = y[:, DN_HEADS * DN_DK + h * DN_DK:DN_HEADS * DN_DK + (h + 1) * DN_DK]
        vh = y[:, 2 * DN_HEADS * DN_DK + h * DN_DV:2 * DN_HEADS * DN_DK + (h + 1) * DN_DV]
        qh = qh * lax.rsqrt(jnp.sum(qh * qh, axis=-1, keepdims=True) + EPS) * (DN_DK ** -0.5)
        kh = kh * lax.rsqrt(jnp.sum(kh * kh, axis=-1, keepdims=True) + EPS)
        for c in range(nch):
            rs = slice(c * DN_CHUNK, (c + 1) * DN_CHUNK)
            gcol = gc_all[rs, MISC_A + h:MISC_A + h + 1]
            grow = gct_all[MISC_A + h:MISC_A + h + 1, rs]
            bcol = beta_all[rs, MISC_B + h:MISC_B + h + 1]
            gamma = jnp.where(incl, jnp.exp(jnp.where(incl, gcol - grow, 0.0)), 0.0)
            qc = qh[rs]
            kc = kh[rs]
            kb = kc * bcol
            eg = jnp.exp(gcol)
            glast = gcol[DN_CHUNK - 1:DN_CHUNK, :]
            items.append(dict(
                h=h, rs=rs,
                a=jnp.where(strict, _dot_nt(kb, kc) * gamma, 0.0),
                rhs=jnp.concatenate([vh[rs] * bcol, kb * eg], axis=1),
                qk=jnp.where(incl, _dot_nt(qc, kc) * gamma, 0.0),
                q_dec=qc * eg,
                k_dec=kc * jnp.exp(glast - gcol),
                cd=jnp.exp(glast),
            ))
    xs = [-it['a'] for it in items]
    ts = [eye + x for x in xs]
    for _ in range(6):
        xs = [_dot(x, x) for x in xs]
        ts = [t + _dot(t, x) for t, x in zip(ts, xs)]
    for it, tinv in zip(items, ts):
        sol = _dot(tinv, it['rhs'])
        qs = _dot(it['qk'], sol)
        ks = _dot_tn(it['k_dec'], sol)
        it['ob'] = qs[:, :DN_DV]
        it['q_eff'] = it['q_dec'] - qs[:, DN_DV:]
        it['s_add'] = ks[:, :DN_DV]
        it['s_mul'] = ks[:, DN_DV:]
    states = [s_scr[h] for h in range(DN_HEADS)]
    for c in range(nch):
        for h in range(DN_HEADS):
            it = items[h * nch + c]
            st = states[h]
            o = _dot(it['q_eff'], st) + it['ob']
            states[h] = it['cd'] * st + it['s_add'] - _dot(it['s_mul'], st)
            on = _rms(o, onorm)
            zz = z_ref[it['rs'], h * DN_DV:(h + 1) * DN_DV]
            o_ref[it['rs'], h * DN_DV:(h + 1) * DN_DV] = (on * _silu(zz)).astype(BF16)
    for h in range(DN_HEADS):
        s_scr[h] = states[h]


def _dn_call(dn_pre, z, misc, lw, batch):
    tp = dn_pre.shape[0]
    tile = ROW_TILE
    tpb = tp // batch // tile
    hb = tile // DN_HALO
    return pl.pallas_call(
        functools.partial(_dn_kernel, tile=tile, tiles_per_batch=tpb),
        grid=(tp // tile,),
        in_specs=[
            pl.BlockSpec((tile, DN_QKV), lambda t: (t, 0)),
            pl.BlockSpec((DN_HALO, DN_QKV), lambda t: (jnp.maximum(t * hb - 1, 0), 0)),
            pl.BlockSpec((tile, DN_HEADS * DN_DV), lambda t: (t, 0)),
            pl.BlockSpec((tile, LANES), lambda t: (t, 0)),
            _const_spec(lw['dn_cw'].shape),
            _const_spec(lw['dn_par'].shape),
            _const_spec(lw['dn_on'].shape),
        ],
        out_specs=pl.BlockSpec((tile, DN_HEADS * DN_DV), lambda t: (t, 0)),
        out_shape=jax.ShapeDtypeStruct((tp, DN_HEADS * DN_DV), BF16),
        scratch_shapes=[
            pltpu.VMEM((DN_HALO + tile, DN_QKV), F32),
            pltpu.VMEM((DN_HEADS, DN_DK, DN_DV), F32),
        ],
        compiler_params=pltpu.CompilerParams(dimension_semantics=("arbitrary",), vmem_limit_bytes=VMEM_LIMIT),
        name="deltanet",
    )(dn_pre, dn_pre, z, misc, lw['dn_cw'], lw['dn_par'], lw['dn_on'])


def _cv_kernel(x_ref, halo_ref, w_ref, b_ref, g_ref, be_ref, o_ref, buf, *, tile, tiles_per_batch):
    t = pl.program_id(0)
    first = (t % tiles_per_batch) == 0
    buf[0:CV_HALO, :] = jnp.where(first, 0.0, halo_ref[...])
    buf[CV_HALO:CV_HALO + tile, :] = x_ref[...]
    base = CV_HALO - CV_WIDTH + 1
    acc = w_ref[0:1, :] * buf[pl.ds(base, tile), :]
    for j in range(1, CV_WIDTH):
        acc = acc + w_ref[j:j + 1, :] * buf[pl.ds(base + j, tile), :]
    acc = acc + b_ref[...]
    o_ref[...] = _silu(_layer_norm(acc, g_ref[...], be_ref[...])).astype(BF16)


def _cv_call(glu, lw, batch):
    tp = glu.shape[0]
    tile = ROW_TILE
    tpb = tp // batch // tile
    hb = tile // CV_HALO
    return pl.pallas_call(
        functools.partial(_cv_kernel, tile=tile, tiles_per_batch=tpb),
        grid=(tp // tile,),
        in_specs=[
            pl.BlockSpec((tile, CV_CH), lambda t: (t, 0)),
            pl.BlockSpec((CV_HALO, CV_CH), lambda t: (jnp.maximum(t * hb - 1, 0), 0)),
            _const_spec(lw['cv_w'].shape),
            _const_spec(lw['cv_b'].shape),
            _const_spec(lw['cv_g'].shape),
            _const_spec(lw['cv_be'].shape),
        ],
        out_specs=pl.BlockSpec((tile, CV_CH), lambda t: (t, 0)),
        out_shape=jax.ShapeDtypeStruct((tp, CV_CH), BF16),
        scratch_shapes=[pltpu.VMEM((CV_HALO + tile, CV_CH), F32)],
        compiler_params=pltpu.CompilerParams(dimension_semantics=("parallel",), vmem_limit_bytes=VMEM_LIMIT),
        name="cv_conv",
    )(glu, glu, lw['cv_w'], lw['cv_b'], lw['cv_g'], lw['cv_be'])


def _merge_kernel(h_ref, om_ref, od_ref, oc_ref, wg_ref, wbm_ref, wbd_ref, wbc_ref, wmo_ref, g_ref, b_ref, wrh_ref,
                  wrl_ref, o_ref, gate_ref, *, tile, tiles_per_batch, seq_rows, route):
    hf = h_ref[...]
    x = hf.astype(BF16)
    merged = None
    for br, (src, wbr) in enumerate(((om_ref, wbm_ref), (od_ref, wbd_ref), (oc_ref, wbc_ref))):
        gl = _dot(x, wg_ref[:, br * D_MODEL:(br + 1) * D_MODEL])
        yb = _dot(src[...], wbr[...])
        term = jax.nn.sigmoid(gl) * yb
        merged = term if merged is None else merged + term
    mix = _dot(merged.astype(BF16), wmo_ref[...])
    h1 = _layer_norm(ALPHA * hf + mix, g_ref[...], b_ref[...])
    row = (pl.program_id(0) % tiles_per_batch) * tile + lax.broadcasted_iota(jnp.int32, (tile, 1), 0)
    live = (row >= FRONT_PAD) & (row < seq_rows)
    h1 = jnp.where(live, h1, 0.0)
    o_ref[...] = h1
    if route:
        h1_hi = h1.astype(BF16)
        h1_lo = (h1 - h1_hi.astype(F32)).astype(BF16)
        logits = _dot(h1_hi, wrh_ref[...]) + _dot(h1_lo, wrh_ref[...]) + _dot(h1_hi, wrl_ref[...])
        lane = lax.broadcasted_iota(jnp.int32, (tile, LANES), 1)
        logits = jnp.where(lane < N_EXPERTS, logits, -jnp.inf)
        m1 = jnp.max(logits, axis=1, keepdims=True)
        i1 = jnp.min(jnp.where(logits == m1, lane, LANES), axis=1, keepdims=True)
        rest = jnp.where(lane == i1, -jnp.inf, logits)
        m2 = jnp.max(rest, axis=1, keepdims=True)
        i2 = jnp.min(jnp.where(rest == m2, lane, LANES), axis=1, keepdims=True)
        e2 = jnp.exp(m2 - m1)
        den = 1.0 + e2
        gate_ref[...] = jnp.where(lane == i1, 1.0 / den, jnp.where(lane == i2, e2 / den, 0.0))
    else:
        gate_ref[...] = jnp.ones(gate_ref.shape, F32)


def _merge_call(h, o_mla, o_dn, o_cv, lw, batch, seq_rows, route):
    tp = h.shape[0]
    tile = ROW_TILE
    tpb = tp // batch // tile
    row = lambda w: pl.BlockSpec((tile, w), lambda i: (i, 0))
    consts = [lw['wg'], lw['wbm'], lw['wbd'], lw['wbc'], lw['wmo'], lw['ln1_g'], lw['ln1_b'], lw['wr_hi'], lw['wr_lo']]
    return pl.pallas_call(
        functools.partial(_merge_kernel, tile=tile, tiles_per_batch=tpb, seq_rows=seq_rows, route=route),
        grid=(tp // tile,),
        in_specs=[row(D_MODEL), row(512), row(512), row(512)] + [_const_spec(a.shape) for a in consts],
        out_specs=[row(D_MODEL), row(LANES)],
        out_shape=[jax.ShapeDtypeStruct((tp, D_MODEL), F32), jax.ShapeDtypeStruct((tp, LANES), F32)],
        compiler_params=pltpu.CompilerParams(dimension_semantics=("parallel",), vmem_limit_bytes=VMEM_LIMIT),
        name="merge_ln1",
    )(h, o_mla, o_dn, o_cv, *consts)


def _ffn_kernel(h_ref, gate_ref, wg_ref, wu_ref, wd_ref, g_ref, b_ref, o_ref, acc, *, tile, tiles_per_batch, seq_rows, gated):
    c = pl.program_id(1)
    x = h_ref[...].astype(BF16)
    hid = _silu(_dot(x, wg_ref[0])) * _dot(x, wu_ref[0])
    y = _dot(hid.astype(BF16), wd_ref[0])
    if gated:
        lane = lax.broadcasted_iota(jnp.int32, (tile, LANES), 1)
        y = y * jnp.sum(jnp.where(lane == c, gate_ref[...], 0.0), axis=1, keepdims=True)

    @pl.when(c == 0)
    def _():
        acc[...] = y

    @pl.when(c > 0)
    def _():
        acc[...] = acc[...] + y

    @pl.when(c == pl.num_programs(1) - 1)
    def _():
        h2 = _layer_norm(ALPHA * h_ref[...] + acc[...], g_ref[...], b_ref[...])
        row = (pl.program_id(0) % tiles_per_batch) * tile + lax.broadcasted_iota(jnp.int32, (tile, 1), 0)
        live = (row >= FRONT_PAD) & (row < seq_rows)
        o_ref[...] = jnp.where(live, h2, 0.0)


def _ffn_call(h, gates, wg, wu, wd, ln_g, ln_b, batch, seq_rows, gated):
    tp = h.shape[0]
    tile = ROW_TILE
    tpb = tp // batch // tile
    nchunk, _, width = wg.shape
    return pl.pallas_call(
        functools.partial(_ffn_kernel, tile=tile, tiles_per_batch=tpb, seq_rows=seq_rows, gated=gated),
        grid=(tp // tile, nchunk),
        in_specs=[
            pl.BlockSpec((tile, D_MODEL), lambda i, c: (i, 0)),
            pl.BlockSpec((tile, LANES), lambda i, c: (i, 0)),
            pl.BlockSpec((1, D_MODEL, width), lambda i, c: (c, 0, 0)),
            pl.BlockSpec((1, D_MODEL, width), lambda i, c: (c, 0, 0)),
            pl.BlockSpec((1, width, D_MODEL), lambda i, c: (c, 0, 0)),
            _const_spec(ln_g.shape),
            _const_spec(ln_b.shape),
        ],
        out_specs=pl.BlockSpec((tile, D_MODEL), lambda i, c: (i, 0)),
        out_shape=jax.ShapeDtypeStruct((tp, D_MODEL), F32),
        scratch_shapes=[pltpu.VMEM((tile, D_MODEL), F32)],
        compiler_params=pltpu.CompilerParams(dimension_semantics=("parallel", "arbitrary"), vmem_limit_bytes=VMEM_LIMIT),
        name="ffn_ln2",
    )(h, gates, wg, wu, wd, ln_g, ln_b)


def _row(v, width=None):
    v = v.astype(F32).reshape(1, -1)
    if width is not None and v.shape[1] < width:
        v = jnp.pad(v, ((0, 0), (0, width - v.shape[1])))
    return v


def _layer_weights(l, w_in, mla_q_norm, mla_w_uq, mla_kv_norm, mla_w_ukv, dn_conv_w, dn_a_log, dn_dt_bias, dn_o_norm,
                   cv_dw_w, cv_dw_b, cv_ln_g, cv_ln_b, w_br_mla, w_br_dn, w_br_cv, w_mix_out, ln1_g, ln1_b, ln2_g, ln2_b,
                   moe_w_router):
    sizes = (MLA_Q_RANK, MLA_KV_RANK, MLA_ROPE, DN_QKV, DN_HEADS * DN_DV, DN_HEADS, DN_HEADS, CV_CH, CV_CH,
             D_MODEL, D_MODEL, D_MODEL)
    offs = np.cumsum((0,) + sizes)
    col = lambda i: w_in[l][:, offs[i]:offs[i + 1]]
    half = MLA_ROPE // 2
    w_kr = col(2)
    w_kr_pair = jnp.concatenate([-w_kr[:, half:], w_kr[:, :half]], axis=1)
    wa = jnp.concatenate([col(0), col(1), w_kr, w_kr_pair, col(5), col(6)], axis=1)
    wa = jnp.pad(wa, ((0, 0), (0, 512 - wa.shape[1])))
    wuq = mla_w_uq[l].reshape(MLA_Q_RANK, MLA_HEADS, MLA_QK)
    zq = jnp.zeros((MLA_Q_RANK, MLA_HEADS, LANES - MLA_QK), F32)
    wq_main = jnp.concatenate([wuq, zq], axis=2)
    wq_pair = jnp.concatenate([jnp.zeros((MLA_Q_RANK, MLA_HEADS, MLA_NOPE), F32), -wuq[:, :, MLA_NOPE + half:],
                               wuq[:, :, MLA_NOPE:MLA_NOPE + half], zq], axis=2)
    wq = jnp.concatenate([wq_main.reshape(MLA_Q_RANK, -1), wq_pair.reshape(MLA_Q_RANK, -1)], axis=1)
    wukv = mla_w_ukv[l].reshape(MLA_KV_RANK, MLA_HEADS, MLA_NOPE + MLA_V)
    wk_lat = jnp.concatenate([wukv[:, :, :MLA_NOPE], jnp.zeros((MLA_KV_RANK, MLA_HEADS, LANES - MLA_NOPE), F32)], axis=2)
    place = np.zeros((LANES, MLA_HEADS, LANES), np.float32)
    for r in range(MLA_ROPE):
        place[r, :, MLA_NOPE + r] = 1.0
    wk = jnp.concatenate([wk_lat.reshape(MLA_KV_RANK, -1), jnp.asarray(place).reshape(LANES, -1)], axis=0)
    wv = jnp.concatenate([wukv[:, :, MLA_NOPE:], jnp.zeros((MLA_KV_RANK, MLA_HEADS, LANES - MLA_V), F32)],
                         axis=2).reshape(MLA_KV_RANK, -1)
    vone = np.zeros((1, MLA_HEADS, LANES), np.float32)
    vone[:, :, MLA_V] = 1.0
    par = jnp.zeros((8, LANES), F32)
    par = par.at[0, MISC_A:MISC_A + DN_HEADS].set(dn_a_log[l].astype(F32))
    par = par.at[1, MISC_A:MISC_A + DN_HEADS].set(dn_dt_bias[l].astype(F32))
    if moe_w_router is None:
        wr = jnp.zeros((D_MODEL, LANES), F32)
    else:
        wr = jnp.pad(moe_w_router.astype(F32), ((0, 0), (0, LANES - N_EXPERTS)))
    wr_hi = wr.astype(BF16)
    wr_lo = (wr - wr_hi.astype(F32)).astype(BF16)
    return dict(
        vone=jnp.asarray(vone.reshape(1, -1)), wr_hi=wr_hi, wr_lo=wr_lo,
        wa=wa.astype(BF16), wq=wq.astype(BF16), wk=wk.astype(BF16), wv=wv.astype(BF16),
        wb=col(3).astype(BF16), wc=col(4).astype(BF16), wd=jnp.concatenate([col(7), col(8)], axis=1).astype(BF16),
        qg=_row(mla_q_norm[l]), kvg=_row(mla_kv_norm[l]),
        dn_cw=jnp.pad(dn_conv_w[l].astype(F32), ((0, 8 - DN_CONV), (0, 0))), dn_par=par, dn_on=_row(dn_o_norm[l]),
        cv_w=jnp.pad(cv_dw_w[l].astype(F32), ((0, 32 - CV_WIDTH), (0, 0))), cv_b=_row(cv_dw_b[l]),
        cv_g=_row(cv_ln_g[l]), cv_be=_row(cv_ln_b[l]),
        wg=jnp.concatenate([col(9), col(10), col(11)], axis=1).astype(BF16),
        wbm=w_br_mla[l].astype(BF16), wbd=w_br_dn[l].astype(BF16), wbc=w_br_cv[l].astype(BF16),
        wmo=w_mix_out[l].astype(BF16), ln1_g=_row(ln1_g[l]), ln1_b=_row(ln1_b[l]),
        ln2_g=_row(ln2_g[l]), ln2_b=_row(ln2_b[l]),
    )


def _rope_tables(lp):
    pos = jnp.maximum(jnp.arange(lp, dtype=jnp.int32) - FRONT_PAD, 0).astype(F32)
    inv_freq = ROPE_THETA ** (-jnp.arange(0, MLA_ROPE, 2, dtype=F32) / MLA_ROPE)
    ang = pos[:, None] * inv_freq[None, :]
    cos2 = jnp.concatenate([jnp.cos(ang), jnp.cos(ang)], axis=1)
    sin2 = jnp.concatenate([jnp.sin(ang), jnp.sin(ang)], axis=1)
    scale = MLA_QK ** -0.5 * np.log2(np.e)
    zpad = jnp.zeros((lp, LANES - MLA_QK), F32)
    ct = jnp.concatenate([jnp.full((lp, MLA_NOPE), scale, F32), scale * cos2, zpad], axis=1)
    st = jnp.concatenate([jnp.zeros((lp, MLA_NOPE), F32), scale * sin2, zpad], axis=1)
    kc = jnp.concatenate([cos2, sin2, jnp.zeros((lp, LANES - 2 * MLA_ROPE), F32)], axis=1)
    return ct, st, kc


def kernel(x, meta_tokens, w_in, mla_q_norm, mla_w_uq, mla_kv_norm, mla_w_ukv, dn_conv_w, dn_a_log, dn_dt_bias, dn_o_norm, cv_dw_w, cv_dw_b, cv_ln_g, cv_ln_b, w_br_mla, w_br_dn, w_br_cv, w_mix_out, ln1_g, ln1_b, ln2_g, ln2_b, ffn_w_gate, ffn_w_up, ffn_w_down, moe_w_router, moe_w_gate, moe_w_up, moe_w_down):
    batch, seq, _ = x.shape
    seq_rows = FRONT_PAD + N_META + seq
    lp = -(-seq_rows // ROW_TILE) * ROW_TILE
    meta = jnp.broadcast_to(meta_tokens[None].astype(x.dtype), (batch, N_META, D_MODEL))
    hp = jnp.concatenate([jnp.zeros((batch, FRONT_PAD, D_MODEL), x.dtype), meta, x,
                          jnp.zeros((batch, lp - seq_rows, D_MODEL), x.dtype)], axis=1)
    h = hp.reshape(batch * lp, D_MODEL)
    ct, st, kc = _rope_tables(lp)
    tabs = dict(ct=jnp.tile(ct, (batch, 1)), st=jnp.tile(st, (batch, 1)), kc=jnp.tile(kc, (batch, 1)))
    for l in range(DEPTH):
        moe = l % 2 == 1
        lw = _layer_weights(l, w_in, mla_q_norm, mla_w_uq, mla_kv_norm, mla_w_ukv, dn_conv_w, dn_a_log, dn_dt_bias,
                            dn_o_norm, cv_dw_w, cv_dw_b, cv_ln_g, cv_ln_b, w_br_mla, w_br_dn, w_br_cv, w_mix_out,
                            ln1_g, ln1_b, ln2_g, ln2_b, moe_w_router[l // 2] if moe else None)
        q, k, v, dn_pre, z, glu, misc = _proj_call(h, lw, tabs)
        o_mla = _attn_call(q, k, v, batch)
        o_dn = _dn_call(dn_pre, z, misc, lw, batch)
        o_cv = _cv_call(glu, lw, batch)
        h1, gates = _merge_call(h, o_mla, o_dn, o_cv, lw, batch, seq_rows, moe)
        if moe:
            wg, wu, wd = moe_w_gate[l // 2], moe_w_up[l // 2], moe_w_down[l // 2]
        else:
            nck = D_FF // D_FF_EXPERT
            wg = ffn_w_gate[l // 2].reshape(D_MODEL, nck, D_FF_EXPERT).transpose(1, 0, 2)
            wu = ffn_w_up[l // 2].reshape(D_MODEL, nck, D_FF_EXPERT).transpose(1, 0, 2)
            wd = ffn_w_down[l // 2].reshape(nck, D_FF_EXPERT, D_MODEL)
        h = _ffn_call(h1, gates, wg.astype(BF16), wu.astype(BF16), wd.astype(BF16), lw['ln2_g'], lw['ln2_b'],
                      batch, seq_rows, moe)
    out = h.reshape(batch, lp, D_MODEL)
    return out[:, FRONT_PAD + N_META:FRONT_PAD + N_META + seq]
```

```python
import functools

import jax
import jax.numpy as jnp
import numpy as np
from jax import lax
from jax.experimental import pallas as pl
from jax.experimental.pallas import tpu as pltpu

D_MODEL = 1024
DEPTH = 2
N_META = 16
MLA_HEADS = 8
MLA_Q_RANK = 256
MLA_KV_RANK = 128
MLA_NOPE = 64
MLA_ROPE = 32
MLA_V = 64
ROPE_THETA = 10000.0
DN_HEADS = 4
DN_DK = 128
DN_DV = 128
DN_CONV = 4
CV_CH = 512
CV_WIDTH = 31
D_FF = 2816
N_EXPERTS = 8
D_FF_EXPERT = 1408
ALPHA = (2 * DEPTH) ** 0.25
EPS = 1e-6
NEG_INF = -1e30
MLA_QK = MLA_NOPE + MLA_ROPE
DN_QKV = 2 * DN_HEADS * DN_DK + DN_HEADS * DN_DV

LANES = 128
SUBLANES = 8
FRONT_PAD = 496
ROW_TILE = 512
DN_CHUNK = 128
CV_HALO = 32
DN_HALO = 8
MISC_B = 64
MISC_A = 68
VMEM_LIMIT = 56 * 1024 * 1024

BF16 = jnp.bfloat16
F32 = jnp.float32


def _dot(a, b):
    return jnp.dot(a, b, preferred_element_type=F32)


def _dot_nt(a, b):
    return lax.dot_general(a, b, (((1,), (1,)), ((), ())), preferred_element_type=F32)


def _dot_tn(a, b):
    return lax.dot_general(a, b, (((0,), (0,)), ((), ())), preferred_element_type=F32)


def _rms(x, g):
    return x * lax.rsqrt(jnp.mean(x * x, axis=-1, keepdims=True) + EPS) * g


def _layer_norm(x, g, b):
    mu = jnp.mean(x, axis=-1, keepdims=True)
    xc = x - mu
    var = jnp.mean(xc * xc, axis=-1, keepdims=True)
    return xc * lax.rsqrt(var + EPS) * g + b


def _silu(x):
    return x * jax.nn.sigmoid(x)


def _const_spec(shape):
    return pl.BlockSpec(shape, lambda *_: (0,) * len(shape))


def _frames_spec(tile, tpb):
    return pl.BlockSpec((tile, D_MODEL), lambda i, *_: ((i // tpb) * (tpb - 1) + jnp.maximum(i % tpb - 1, 0), 0))


def _load_rows(h_ref, head_ref, tiles_per_batch):
    if head_ref is None:
        return h_ref[...]
    return jnp.where(pl.program_id(0) % tiles_per_batch == 0, head_ref[...], h_ref[...])


def _proj_kernel(*refs, tiles_per_batch, from_frames):
    h_ref, head_ref = (refs[0], refs[1]) if from_frames else (refs[0], None)
    (wa_ref, wq_ref, wk_ref, wv_ref, wb_ref, wc_ref, wd_ref, qg_ref, kvg_ref, vone_ref, ct_ref, st_ref, kc_ref,
     q_ref, k_ref, v_ref, dn_ref, z_ref, glu_ref, misc_ref) = refs[2 if from_frames else 1:]
    x = _load_rows(h_ref, head_ref, tiles_per_batch).astype(BF16)
    pa = _dot(x, wa_ref[...])
    cq = pa[:, :MLA_Q_RANK]
    ckv = pa[:, MLA_Q_RANK:MLA_Q_RANK + MLA_KV_RANK]
    blk = pa[:, MLA_Q_RANK + MLA_KV_RANK:]
    cqn = _rms(cq, qg_ref[...])
    qq = _dot(cqn.astype(BF16), wq_ref[...])
    ct = ct_ref[...]
    st = st_ref[...]
    hw = MLA_HEADS * LANES
    for h in range(MLA_HEADS):
        sl = slice(h * LANES, (h + 1) * LANES)
        sp = slice(hw + h * LANES, hw + (h + 1) * LANES)
        q_ref[:, sl] = (qq[:, sl] * ct + qq[:, sp] * st).astype(BF16)
    ckvn = _rms(ckv, kvg_ref[...])
    prod = blk * kc_ref[...]
    krr = prod + pltpu.roll(prod, LANES - MLA_ROPE, axis=1)
    kin = jnp.concatenate([ckvn, krr], axis=1).astype(BF16)
    k_ref[...] = _dot(kin, wk_ref[...]).astype(BF16)
    v_ref[...] = (_dot(ckvn.astype(BF16), wv_ref[...]) + vone_ref[...]).astype(BF16)
    dn_ref[...] = _dot(x, wb_ref[...])
    z_ref[...] = _dot(x, wc_ref[...])
    cv = _dot(x, wd_ref[...])
    glu_ref[...] = cv[:, :CV_CH] * jax.nn.sigmoid(cv[:, CV_CH:])
    misc_ref[...] = blk


def _proj_call(h, head, lw, tabs, tp, batch):
    tm = ROW_TILE
    tpb = tp // batch // tm
    row = lambda w: pl.BlockSpec((tm, w), lambda i: (i, 0))
    consts = [lw['wa'], lw['wq'], lw['wk'], lw['wv'], lw['wb'], lw['wc'], lw['wd'], lw['qg'], lw['kvg'], lw['vone']]
    if head is None:
        ins, in_specs = [h], [row(D_MODEL)]
    else:
        ins, in_specs = [h, head], [_frames_spec(tm, tpb), _const_spec(head.shape)]
    ins += consts + [tabs['ct'], tabs['st'], tabs['kc']]
    in_specs += [_const_spec(a.shape) for a in consts] + [row(LANES)] * 3
    out_shape = [
        jax.ShapeDtypeStruct((tp, MLA_HEADS * LANES), BF16),
        jax.ShapeDtypeStruct((tp, MLA_HEADS * LANES), BF16),
        jax.ShapeDtypeStruct((tp, MLA_HEADS * LANES), BF16),
        jax.ShapeDtypeStruct((tp, DN_QKV), F32),
        jax.ShapeDtypeStruct((tp, DN_HEADS * DN_DV), F32),
        jax.ShapeDtypeStruct((tp, CV_CH), F32),
        jax.ShapeDtypeStruct((tp, LANES), F32),
    ]
    out_specs = [row(s.shape[1]) for s in out_shape]
    return pl.pallas_call(
        functools.partial(_proj_kernel, tiles_per_batch=tpb, from_frames=head is not None),
        grid=(tp // tm,),
        in_specs=in_specs,
        out_specs=out_specs,
        out_shape=out_shape,
        compiler_params=pltpu.CompilerParams(dimension_semantics=("parallel",), vmem_limit_bytes=VMEM_LIMIT),
        name="proj",
    )(*ins)


def _attn_kernel(it_ref, jt_ref, q_ref, ka_ref, va_ref, kb_ref, vb_ref, *rest, tile, rq):
    o_ref, m_scr, acc_scr = rest[-3:]
    step = pl.program_id(1)
    i = it_ref[step]
    last = i + rq - 1
    ja = jt_ref[step]
    rows = rq * tile

    @pl.when(ja == 0)
    def _():
        m_scr[...] = jnp.full(m_scr.shape, NEG_INF, F32)
        acc_scr[...] = jnp.zeros(acc_scr.shape, F32)

    def body(r0, r1, tiles):
        keeps = []
        for _, _, jb, masked in tiles:
            if masked:
                qpos = i * tile + r0 + lax.broadcasted_iota(jnp.int32, (r1 - r0, tile), 0)
                kpos = jb * tile + lax.broadcasted_iota(jnp.int32, (r1 - r0, tile), 1)
                keeps.append((kpos <= qpos) & (kpos >= FRONT_PAD))
            else:
                keeps.append(None)
        for h in range(MLA_HEADS):
            sl = slice(h * LANES, (h + 1) * LANES)
            q = q_ref[r0:r1, sl]
            ss = []
            for (kr, _, _, _), keep in zip(tiles, keeps):
                s = _dot_nt(q, kr[:, sl])
                ss.append(s if keep is None else jnp.where(keep, s, NEG_INF))
            m_prev = m_scr[h, r0:r1]
            m_cur = jnp.max(ss[0], axis=1, keepdims=True)
            for s in ss[1:]:
                m_cur = jnp.maximum(m_cur, jnp.max(s, axis=1, keepdims=True))
            m_new = jnp.maximum(m_prev, m_cur)
            m_scr[h, r0:r1] = m_new
            pv = None
            for (_, vr, _, _), s in zip(tiles, ss):
                t = _dot(jnp.exp2(s - m_new[:, :1]).astype(BF16), vr[:, sl])
                pv = t if pv is None else pv + t
            acc_scr[h, r0:r1] = jnp.exp2(m_prev - m_new) * acc_scr[h, r0:r1] + pv

    ta = (ka_ref, va_ref, ja)
    tb = (kb_ref, vb_ref, ja + 1)
    if rq == 1:
        two = ja + 1 <= i
        edge = (ja == 0) | (ja + 1 >= i)
        pl.when(two & jnp.logical_not(edge))(lambda: body(0, rows, [ta + (False,), tb + (False,)]))
        pl.when(two & edge)(lambda: body(0, rows, [ta + (True,), tb + (True,)]))
        pl.when(jnp.logical_not(two))(lambda: body(0, rows, [ta + (True,)]))
    else:
        diag = ja == i
        pl.when(jnp.logical_not(diag) & (ja > 0))(lambda: body(0, rows, [ta + (False,), tb + (False,)]))
        pl.when(jnp.logical_not(diag) & (ja == 0))(lambda: body(0, rows, [ta + (True,), tb + (False,)]))

        @pl.when(diag)
        def _():
            body(0, tile, [ta + (True,)])
            body(tile, rows, [ta + (True,), tb + (True,)])

    @pl.when(ja + 1 >= last)
    def _():
        lane = lax.broadcasted_iota(jnp.int32, (rows, LANES), 1)
        for hp in range(MLA_HEADS // 2):
            a0 = acc_scr[2 * hp]
            a1 = acc_scr[2 * hp + 1]
            lo = a0 / a0[:, MLA_V:MLA_V + 1]
            hi = pltpu.roll(a1 / a1[:, MLA_V:MLA_V + 1], MLA_V, axis=1)
            o_ref[:, hp * LANES:(hp + 1) * LANES] = jnp.where(lane < MLA_V, lo, hi).astype(BF16)


def _attn_call(q, k, v, batch):
    tp = q.shape[0]
    lp = tp // batch
    tile = ROW_TILE
    nb = lp // tile
    qw = MLA_HEADS * LANES
    ow = MLA_HEADS * MLA_V
    q3, k3, v3 = (a.reshape(batch, lp, qw) for a in (q, k, v))

    def call(rq, q_tiles, prev):
        steps = [(i, ja) for i in q_tiles for ja in range(0, i + rq, 2)]
        it = jnp.asarray(np.array([p[0] for p in steps], np.int32))
        jt = jnp.asarray(np.array([p[1] for p in steps], np.int32))
        qmap = lambda b, s, it, jt: (b, it[s] // rq, 0)
        first = lambda b, s, it, jt: (b, jt[s], 0)
        second = lambda b, s, it, jt: (b, jnp.minimum(jt[s] + 1, it[s] + rq - 1), 0)
        in_specs = [
            pl.BlockSpec((None, rq * tile, qw), qmap),
            pl.BlockSpec((None, tile, qw), first),
            pl.BlockSpec((None, tile, qw), first),
            pl.BlockSpec((None, tile, qw), second),
            pl.BlockSpec((None, tile, qw), second),
        ]
        args = [it, jt, q3, k3, v3, k3, v3]
        aliases = {}
        if prev is not None:
            in_specs.append(pl.BlockSpec(memory_space=pl.ANY))
            args.append(prev)
            aliases = {len(args) - 1: 0}
        grid_spec = pltpu.PrefetchScalarGridSpec(
            num_scalar_prefetch=2,
            grid=(batch, len(steps)),
            in_specs=in_specs,
            out_specs=pl.BlockSpec((None, rq * tile, ow), qmap),
            scratch_shapes=[
                pltpu.VMEM((MLA_HEADS, rq * tile, LANES), F32),
                pltpu.VMEM((MLA_HEADS, rq * tile, LANES), F32),
            ],
        )
        return pl.pallas_call(
            functools.partial(_attn_kernel, tile=tile, rq=rq),
            grid_spec=grid_spec,
            out_shape=jax.ShapeDtypeStruct((batch, lp, ow), BF16),
            input_output_aliases=aliases,
            compiler_params=pltpu.CompilerParams(dimension_semantics=("parallel", "arbitrary"),
                                                 vmem_limit_bytes=VMEM_LIMIT),
            name="mla_attn_x%d" % rq,
        )(*args)

    out = None
    if nb >= 2:
        out = call(2, list(range(0, nb - nb % 2, 2)), None)
    if nb % 2:
        out = call(1, [nb - 1], out)
    return out.reshape(tp, ow)


def _dn_kernel(x_ref, halo_ref, z_ref, misc_ref, cw_ref, par_ref, on_ref, o_ref, buf, s_scr, *, tile, tiles_per_batch):
    t = pl.program_id(0)
    tb = t % tiles_per_batch
    first = tb == 0

    @pl.when(first)
    def _():
        s_scr[...] = jnp.zeros(s_scr.shape, F32)

    buf[0:DN_HALO, :] = jnp.where(first, 0.0, halo_ref[...])
    buf[DN_HALO:DN_HALO + tile, :] = x_ref[...]
    y = cw_ref[0:1, :] * buf[pl.ds(DN_HALO - DN_CONV + 1, tile), :]
    for jj in range(1, DN_CONV):
        y = y + cw_ref[jj:jj + 1, :] * buf[pl.ds(DN_HALO - DN_CONV + 1 + jj, tile), :]
    y = _silu(y)

    misc = misc_ref[...]
    row = tb * tile + lax.broadcasted_iota(jnp.int32, (tile, LANES), 0)
    valid = row >= FRONT_PAD
    beta_all = jnp.where(valid, jax.nn.sigmoid(misc), 0.0)
    g_all = jnp.where(valid, -jnp.exp(par_ref[0:1, :]) * jax.nn.softplus(misc + par_ref[1:2, :]), 0.0)
    rin = lax.broadcasted_iota(jnp.int32, (tile, LANES), 0) % DN_CHUNK
    gc_all = g_all
    sh = 1
    while sh < DN_CHUNK:
        gc_all = gc_all + jnp.where(rin >= sh, pltpu.roll(gc_all, sh, axis=0), 0.0)
        sh *= 2
    gct_all = gc_all.T

    ii = lax.broadcasted_iota(jnp.int32, (DN_CHUNK, DN_CHUNK), 0)
    jj_ = lax.broadcasted_iota(jnp.int32, (DN_CHUNK, DN_CHUNK), 1)
    incl = ii >= jj_
    strict = ii > jj_
    eye = jnp.where(ii == jj_, 1.0, 0.0).astype(F32)
    onorm = on_ref[...]

    nch = tile // DN_CHUNK
    items = []
    for h in range(DN_HEADS):
        hs = slice(h * DN_DK, (h + 1) * DN_DK)
        qh = y[:, hs]
        kh = y[:, DN_HEADS * DN_DK + h * DN_DK:DN_HEADS * DN_DK + (h + 1) * DN_DK]
        vh = y[:, 2 * DN_HEADS * DN_DK + h * DN_DV:2 * DN_HEADS * DN_DK + (h + 1) * DN_DV]
        qh = qh * lax.rsqrt(jnp.sum(qh * qh, axis=-1, keepdims=True) + EPS) * (DN_DK ** -0.5)
        kh = kh * lax.rsqrt(jnp.sum(kh * kh, axis=-1, keepdims=True) + EPS)
        for c in range(nch):
            rs = slice(c * DN_CHUNK, (c + 1) * DN_CHUNK)
            gcol = gc_all[rs, MISC_A + h:MISC_A + h + 1]
            grow = gct_all[MISC_A + h:MISC_A + h + 1, rs]
            bcol = beta_all[rs, MISC_B + h:MISC_B + h + 1]
            gamma = jnp.where(incl, jnp.exp(jnp.where(incl, gcol - grow, 0.0)), 0.0)
            qc = qh[rs]
            kc = kh[rs]
            kb = kc * bcol
            eg = jnp.exp(gcol)
            glast = gcol[DN_CHUNK - 1:DN_CHUNK, :]
            items.append(dict(
                h=h, rs=rs,
                a=jnp.where(strict, _dot_nt(kb, kc) * gamma, 0.0),
                rhs=jnp.concatenate([vh[rs] * bcol, kb * eg], axis=1),
                qk=jnp.where(incl, _dot_nt(qc, kc) * gamma, 0.0),
                q_dec=qc * eg,
                k_dec=kc * jnp.exp(glast - gcol),
                cd=jnp.exp(glast),
            ))
    xs = [-it['a'] for it in items]
    ts = [eye + x for x in xs]
    for _ in range(6):
        xs = [_dot(x, x) for x in xs]
        ts = [t + _dot(t, x) for t, x in zip(ts, xs)]
    for it, tinv in zip(items, ts):
        sol = _dot(tinv, it['rhs'])
        qs = _dot(it['qk'], sol)
        ks = _dot_tn(it['k_dec'], sol)
        it['ob'] = qs[:, :DN_DV]
        it['q_eff'] = it['q_dec'] - qs[:, DN_DV:]
        it['s_add'] = ks[:, :DN_DV]
        it['s_mul'] = ks[:, DN_DV:]
    states = [s_scr[h] for h in range(DN_HEADS)]
    for c in range(nch):
        for h in range(DN_HEADS):
            it = items[h * nch + c]
            st = states[h]
            o = _dot(it['q_eff'], st) + it['ob']
            states[h] = it['cd'] * st + it['s_add'] - _dot(it['s_mul'], st)
            on = _rms(o, onorm)
            zz = z_ref[it['rs'], h * DN_DV:(h + 1) * DN_DV]
            o_ref[it['rs'], h * DN_DV:(h + 1) * DN_DV] = (on * _silu(zz)).astype(BF16)
    for h in range(DN_HEADS):
        s_scr[h] = states[h]


def _dn_call(dn_pre, z, misc, lw, batch):
    tp = dn_pre.shape[0]
    tile = ROW_TILE
    tpb = tp // batch // tile
    hb = tile // DN_HALO
    return pl.pallas_call(
        functools.partial(_dn_kernel, tile=tile, tiles_per_batch=tpb),
        grid=(tp // tile,),
        in_specs=[
            pl.BlockSpec((tile, DN_QKV), lambda t: (t, 0)),
            pl.BlockSpec((DN_HALO, DN_QKV), lambda t: (jnp.maximum(t * hb - 1, 0), 0)),
            pl.BlockSpec((tile, DN_HEADS * DN_DV), lambda t: (t, 0)),
            pl.BlockSpec((tile, LANES), lambda t: (t, 0)),
            _const_spec(lw['dn_cw'].shape),
            _const_spec(lw['dn_par'].shape),
            _const_spec(lw['dn_on'].shape),
        ],
        out_specs=pl.BlockSpec((tile, DN_HEADS * DN_DV), lambda t: (t, 0)),
        out_shape=jax.ShapeDtypeStruct((tp, DN_HEADS * DN_DV), BF16),
        scratch_shapes=[
            pltpu.VMEM((DN_HALO + tile, DN_QKV), F32),
            pltpu.VMEM((DN_HEADS, DN_DK, DN_DV), F32),
        ],
        compiler_params=pltpu.CompilerParams(dimension_semantics=("arbitrary",), vmem_limit_bytes=VMEM_LIMIT),
        name="deltanet",
    )(dn_pre, dn_pre, z, misc, lw['dn_cw'], lw['dn_par'], lw['dn_on'])


def _cv_kernel(x_ref, halo_ref, w_ref, b_ref, g_ref, be_ref, o_ref, buf, *, tile, tiles_per_batch):
    t = pl.program_id(0)
    first = (t % tiles_per_batch) == 0
    buf[0, 0:CV_HALO, :] = jnp.where(first, 0.0, halo_ref[...])
    buf[0, CV_HALO:CV_HALO + tile, :] = x_ref[...]
    span = CV_HALO + tile - SUBLANES
    for r in range(1, SUBLANES):
        buf[r, 0:span, :] = buf[0, pl.ds(r, span), :]
    base = CV_HALO - CV_WIDTH + 1
    acc = None
    for j in range(CV_WIDTH):
        r = (base + j) % SUBLANES
        term = w_ref[j:j + 1, :] * buf[r, pl.ds(base + j - r, tile), :]
        acc = term if acc is None else acc + term
    acc = acc + b_ref[...]
    o_ref[...] = _silu(_layer_norm(acc, g_ref[...], be_ref[...])).astype(BF16)


def _cv_call(glu, lw, batch):
    tp = glu.shape[0]
    tile = ROW_TILE
    tpb = tp // batch // tile
    hb = tile // CV_HALO
    return pl.pallas_call(
        functools.partial(_cv_kernel, tile=tile, tiles_per_batch=tpb),
        grid=(tp // tile,),
        in_specs=[
            pl.BlockSpec((tile, CV_CH), lambda t: (t, 0)),
            pl.BlockSpec((CV_HALO, CV_CH), lambda t: (jnp.maximum(t * hb - 1, 0), 0)),
            _const_spec(lw['cv_w'].shape),
            _const_spec(lw['cv_b'].shape),
            _const_spec(lw['cv_g'].shape),
            _const_spec(lw['cv_be'].shape),
        ],
        out_specs=pl.BlockSpec((tile, CV_CH), lambda t: (t, 0)),
        out_shape=jax.ShapeDtypeStruct((tp, CV_CH), BF16),
        scratch_shapes=[pltpu.VMEM((SUBLANES, CV_HALO + tile, CV_CH), F32)],
        compiler_params=pltpu.CompilerParams(dimension_semantics=("parallel",), vmem_limit_bytes=VMEM_LIMIT),
        name="cv_conv",
    )(glu, glu, lw['cv_w'], lw['cv_b'], lw['cv_g'], lw['cv_be'])


def _merge_kernel(*refs, tile, tiles_per_batch, seq_rows, route, from_frames):
    h_ref, head_ref = (refs[0], refs[1]) if from_frames else (refs[0], None)
    (om_ref, od_ref, oc_ref, wg_ref, wbm_ref, wbd_ref, wbc_ref, wmo_ref, g_ref, b_ref, wrh_ref, wrl_ref,
     o_ref, gate_ref) = refs[2 if from_frames else 1:]
    hf = _load_rows(h_ref, head_ref, tiles_per_batch)
    x = hf.astype(BF16)
    merged = None
    for br, (src, wbr) in enumerate(((om_ref, wbm_ref), (od_ref, wbd_ref), (oc_ref, wbc_ref))):
        gl = _dot(x, wg_ref[:, br * D_MODEL:(br + 1) * D_MODEL])
        yb = _dot(src[...], wbr[...])
        term = jax.nn.sigmoid(gl) * yb
        merged = term if merged is None else merged + term
    mix = _dot(merged.astype(BF16), wmo_ref[...])
    h1 = _layer_norm(ALPHA * hf + mix, g_ref[...], b_ref[...])
    row = (pl.program_id(0) % tiles_per_batch) * tile + lax.broadcasted_iota(jnp.int32, (tile, 1), 0)
    live = (row >= FRONT_PAD) & (row < seq_rows)
    h1 = jnp.where(live, h1, 0.0)
    o_ref[...] = h1
    if route:
        h1_hi = h1.astype(BF16)
        h1_lo = (h1 - h1_hi.astype(F32)).astype(BF16)
        hh = _dot(h1_hi, wrl_ref[...])
        logits = hh[:, :LANES] + hh[:, LANES:] + _dot(h1_lo, wrh_ref[...])
        lane = lax.broadcasted_iota(jnp.int32, (tile, LANES), 1)
        logits = jnp.where(lane < N_EXPERTS, logits, -jnp.inf)
        m1 = jnp.max(logits, axis=1, keepdims=True)
        i1 = jnp.min(jnp.where(logits == m1, lane, LANES), axis=1, keepdims=True)
        rest = jnp.where(lane == i1, -jnp.inf, logits)
        m2 = jnp.max(rest, axis=1, keepdims=True)
        i2 = jnp.min(jnp.where(rest == m2, lane, LANES), axis=1, keepdims=True)
        e2 = jnp.exp(m2 - m1)
        den = 1.0 + e2
        gate_ref[...] = jnp.where(lane == i1, 1.0 / den, jnp.where(lane == i2, e2 / den, 0.0))
    else:
        gate_ref[...] = jnp.ones(gate_ref.shape, F32)


def _merge_call(h, head, o_mla, o_dn, o_cv, lw, batch, seq_rows, route):
    tp = o_mla.shape[0]
    tile = ROW_TILE
    tpb = tp // batch // tile
    row = lambda w: pl.BlockSpec((tile, w), lambda i: (i, 0))
    consts = [lw['wg'], lw['wbm'], lw['wbd'], lw['wbc'], lw['wmo'], lw['ln1_g'], lw['ln1_b'], lw['wr_hi'], lw['wr_lo']]
    if head is None:
        ins, in_specs = [h], [row(D_MODEL)]
    else:
        ins, in_specs = [h, head], [_frames_spec(tile, tpb), _const_spec(head.shape)]
    return pl.pallas_call(
        functools.partial(_merge_kernel, tile=tile, tiles_per_batch=tpb, seq_rows=seq_rows, route=route,
                          from_frames=head is not None),
        grid=(tp // tile,),
        in_specs=in_specs + [row(512), row(512), row(512)] + [_const_spec(a.shape) for a in consts],
        out_specs=[row(D_MODEL), row(LANES)],
        out_shape=[jax.ShapeDtypeStruct((tp, D_MODEL), F32), jax.ShapeDtypeStruct((tp, LANES), F32)],
        compiler_params=pltpu.CompilerParams(dimension_semantics=("parallel",), vmem_limit_bytes=VMEM_LIMIT),
        name="merge_ln1",
    )(*ins, o_mla, o_dn, o_cv, *consts)


def _ffn_kernel(h_ref, gate_ref, wg_ref, wu_ref, wd_ref, g_ref, b_ref, o_ref, acc, *, tile, tiles_per_batch, seq_rows, gated):
    c = pl.program_id(1)
    x = h_ref[...].astype(BF16)
    hid = _silu(_dot(x, wg_ref[0])) * _dot(x, wu_ref[0])
    y = _dot(hid.astype(BF16), wd_ref[0])
    if gated:
        lane = lax.broadcasted_iota(jnp.int32, (tile, LANES), 1)
        y = y * jnp.sum(jnp.where(lane == c, gate_ref[...], 0.0), axis=1, keepdims=True)

    @pl.when(c == 0)
    def _():
        acc[...] = y

    @pl.when(c > 0)
    def _():
        acc[...] = acc[...] + y

    @pl.when(c == pl.num_programs(1) - 1)
    def _():
        h2 = _layer_norm(ALPHA * h_ref[...] + acc[...], g_ref[...], b_ref[...])
        row = (pl.program_id(0) % tiles_per_batch) * tile + lax.broadcasted_iota(jnp.int32, (tile, 1), 0)
        live = (row >= FRONT_PAD) & (row < seq_rows)
        o_ref[...] = jnp.where(live, h2, 0.0)


def _ffn_call(h, gates, wg, wu, wd, ln_g, ln_b, batch, seq_rows, gated, to_frames):
    tp = h.shape[0]
    tile = ROW_TILE
    tpb = tp // batch // tile
    nchunk, _, width = wg.shape
    if to_frames:
        out_spec = _frames_spec(tile, tpb)
        out_rows = tp - batch * tile
        semantics = ("arbitrary", "arbitrary")
    else:
        out_spec = pl.BlockSpec((tile, D_MODEL), lambda i, c: (i, 0))
        out_rows = tp
        semantics = ("parallel", "arbitrary")
    return pl.pallas_call(
        functools.partial(_ffn_kernel, tile=tile, tiles_per_batch=tpb, seq_rows=seq_rows, gated=gated),
        grid=(tp // tile, nchunk),
        in_specs=[
            pl.BlockSpec((tile, D_MODEL), lambda i, c: (i, 0)),
            pl.BlockSpec((tile, LANES), lambda i, c: (i, 0)),
            pl.BlockSpec((1, D_MODEL, width), lambda i, c: (c, 0, 0)),
            pl.BlockSpec((1, D_MODEL, width), lambda i, c: (c, 0, 0)),
            pl.BlockSpec((1, width, D_MODEL), lambda i, c: (c, 0, 0)),
            _const_spec(ln_g.shape),
            _const_spec(ln_b.shape),
        ],
        out_specs=out_spec,
        out_shape=jax.ShapeDtypeStruct((out_rows, D_MODEL), F32),
        scratch_shapes=[pltpu.VMEM((tile, D_MODEL), F32)],
        compiler_params=pltpu.CompilerParams(dimension_semantics=semantics, vmem_limit_bytes=VMEM_LIMIT),
        name="ffn_ln2",
    )(h, gates, wg, wu, wd, ln_g, ln_b)


def _row(v, width=None):
    v = v.astype(F32).reshape(1, -1)
    if width is not None and v.shape[1] < width:
        v = jnp.pad(v, ((0, 0), (0, width - v.shape[1])))
    return v


def _layer_weights(l, w_in, mla_q_norm, mla_w_uq, mla_kv_norm, mla_w_ukv, dn_conv_w, dn_a_log, dn_dt_bias, dn_o_norm,
                   cv_dw_w, cv_dw_b, cv_ln_g, cv_ln_b, w_br_mla, w_br_dn, w_br_cv, w_mix_out, ln1_g, ln1_b, ln2_g, ln2_b,
                   moe_w_router):
    sizes = (MLA_Q_RANK, MLA_KV_RANK, MLA_ROPE, DN_QKV, DN_HEADS * DN_DV, DN_HEADS, DN_HEADS, CV_CH, CV_CH,
             D_MODEL, D_MODEL, D_MODEL)
    offs = np.cumsum((0,) + sizes)
    col = lambda i: w_in[l][:, offs[i]:offs[i + 1]]
    half = MLA_ROPE // 2
    w_kr = col(2)
    w_kr_pair = jnp.concatenate([-w_kr[:, half:], w_kr[:, :half]], axis=1)
    wa = jnp.concatenate([col(0), col(1), w_kr, w_kr_pair, col(5), col(6)], axis=1)
    wa = jnp.pad(wa, ((0, 0), (0, 512 - wa.shape[1])))
    wuq = mla_w_uq[l].reshape(MLA_Q_RANK, MLA_HEADS, MLA_QK)
    zq = jnp.zeros((MLA_Q_RANK, MLA_HEADS, LANES - MLA_QK), F32)
    wq_main = jnp.concatenate([wuq, zq], axis=2)
    wq_pair = jnp.concatenate([jnp.zeros((MLA_Q_RANK, MLA_HEADS, MLA_NOPE), F32), -wuq[:, :, MLA_NOPE + half:],
                               wuq[:, :, MLA_NOPE:MLA_NOPE + half], zq], axis=2)
    wq = jnp.concatenate([wq_main.reshape(MLA_Q_RANK, -1), wq_pair.reshape(MLA_Q_RANK, -1)], axis=1)
    wukv = mla_w_ukv[l].reshape(MLA_KV_RANK, MLA_HEADS, MLA_NOPE + MLA_V)
    wk_lat = jnp.concatenate([wukv[:, :, :MLA_NOPE], jnp.zeros((MLA_KV_RANK, MLA_HEADS, LANES - MLA_NOPE), F32)], axis=2)
    place = np.zeros((LANES, MLA_HEADS, LANES), np.float32)
    for r in range(MLA_ROPE):
        place[r, :, MLA_NOPE + r] = 1.0
    wk = jnp.concatenate([wk_lat.reshape(MLA_KV_RANK, -1), jnp.asarray(place).reshape(LANES, -1)], axis=0)
    wv = jnp.concatenate([wukv[:, :, MLA_NOPE:], jnp.zeros((MLA_KV_RANK, MLA_HEADS, LANES - MLA_V), F32)],
                         axis=2).reshape(MLA_KV_RANK, -1)
    vone = np.zeros((1, MLA_HEADS, LANES), np.float32)
    vone[:, :, MLA_V] = 1.0
    par = jnp.zeros((8, LANES), F32)
    par = par.at[0, MISC_A:MISC_A + DN_HEADS].set(dn_a_log[l].astype(F32))
    par = par.at[1, MISC_A:MISC_A + DN_HEADS].set(dn_dt_bias[l].astype(F32))
    if moe_w_router is None:
        wr = jnp.zeros((D_MODEL, LANES), F32)
    else:
        wr = jnp.pad(moe_w_router.astype(F32), ((0, 0), (0, LANES - N_EXPERTS)))
    wr_hi = wr.astype(BF16)
    wr_lo = jnp.concatenate([wr_hi, (wr - wr_hi.astype(F32)).astype(BF16)], axis=1)
    return dict(
        vone=jnp.asarray(vone.reshape(1, -1)), wr_hi=wr_hi, wr_lo=wr_lo,
        wa=wa.astype(BF16), wq=wq.astype(BF16), wk=wk.astype(BF16), wv=wv.astype(BF16),
        wb=col(3).astype(BF16), wc=col(4).astype(BF16), wd=jnp.concatenate([col(7), col(8)], axis=1).astype(BF16),
        qg=_row(mla_q_norm[l]), kvg=_row(mla_kv_norm[l]),
        dn_cw=jnp.pad(dn_conv_w[l].astype(F32), ((0, 8 - DN_CONV), (0, 0))), dn_par=par, dn_on=_row(dn_o_norm[l]),
        cv_w=jnp.pad(cv_dw_w[l].astype(F32), ((0, 32 - CV_WIDTH), (0, 0))), cv_b=_row(cv_dw_b[l]),
        cv_g=_row(cv_ln_g[l]), cv_be=_row(cv_ln_b[l]),
        wg=jnp.concatenate([col(9), col(10), col(11)], axis=1).astype(BF16),
        wbm=w_br_mla[l].astype(BF16), wbd=w_br_dn[l].astype(BF16), wbc=w_br_cv[l].astype(BF16),
        wmo=w_mix_out[l].astype(BF16), ln1_g=_row(ln1_g[l]), ln1_b=_row(ln1_b[l]),
        ln2_g=_row(ln2_g[l]), ln2_b=_row(ln2_b[l]),
    )


def _rope_tables(lp):
    pos = jnp.maximum(jnp.arange(lp, dtype=jnp.int32) - FRONT_PAD, 0).astype(F32)
    inv_freq = ROPE_THETA ** (-jnp.arange(0, MLA_ROPE, 2, dtype=F32) / MLA_ROPE)
    ang = pos[:, None] * inv_freq[None, :]
    cos2 = jnp.concatenate([jnp.cos(ang), jnp.cos(ang)], axis=1)
    sin2 = jnp.concatenate([jnp.sin(ang), jnp.sin(ang)], axis=1)
    scale = MLA_QK ** -0.5 * np.log2(np.e)
    zpad = jnp.zeros((lp, LANES - MLA_QK), F32)
    ct = jnp.concatenate([jnp.full((lp, MLA_NOPE), scale, F32), scale * cos2, zpad], axis=1)
    st = jnp.concatenate([jnp.zeros((lp, MLA_NOPE), F32), scale * sin2, zpad], axis=1)
    kc = jnp.concatenate([cos2, sin2, jnp.zeros((lp, LANES - 2 * MLA_ROPE), F32)], axis=1)
    return ct, st, kc


def kernel(x, meta_tokens, w_in, mla_q_norm, mla_w_uq, mla_kv_norm, mla_w_ukv, dn_conv_w, dn_a_log, dn_dt_bias, dn_o_norm, cv_dw_w, cv_dw_b, cv_ln_g, cv_ln_b, w_br_mla, w_br_dn, w_br_cv, w_mix_out, ln1_g, ln1_b, ln2_g, ln2_b, ffn_w_gate, ffn_w_up, ffn_w_down, moe_w_router, moe_w_gate, moe_w_up, moe_w_down):
    batch, seq, _ = x.shape
    assert FRONT_PAD + N_META == ROW_TILE and seq % ROW_TILE == 0 and seq >= ROW_TILE
    lp = seq_rows = ROW_TILE + seq
    tp = batch * lp
    head = jnp.concatenate([jnp.zeros((FRONT_PAD, D_MODEL), F32), meta_tokens.astype(F32)], axis=0)
    h = x.astype(F32).reshape(batch * seq, D_MODEL)
    ct, st, kc = _rope_tables(lp)
    tabs = dict(ct=jnp.tile(ct, (batch, 1)), st=jnp.tile(st, (batch, 1)), kc=jnp.tile(kc, (batch, 1)))
    for l in range(DEPTH):
        moe = l % 2 == 1
        lw = _layer_weights(l, w_in, mla_q_norm, mla_w_uq, mla_kv_norm, mla_w_ukv, dn_conv_w, dn_a_log, dn_dt_bias,
                            dn_o_norm, cv_dw_w, cv_dw_b, cv_ln_g, cv_ln_b, w_br_mla, w_br_dn, w_br_cv, w_mix_out,
                            ln1_g, ln1_b, ln2_g, ln2_b, moe_w_router[l // 2] if moe else None)
        src_head = head if l == 0 else None
        q, k, v, dn_pre, z, glu, misc = _proj_call(h, src_head, lw, tabs, tp, batch)
        o_mla = _attn_call(q, k, v, batch)
        o_dn = _dn_call(dn_pre, z, misc, lw, batch)
        o_cv = _cv_call(glu, lw, batch)
        h1, gates = _merge_call(h, src_head, o_mla, o_dn, o_cv, lw, batch, seq_rows, moe)
        if moe:
            wg, wu, wd = moe_w_gate[l // 2], moe_w_up[l // 2], moe_w_down[l // 2]
        else:
            nck = D_FF // D_FF_EXPERT
            wg = ffn_w_gate[l // 2].reshape(D_MODEL, nck, D_FF_EXPERT).transpose(1, 0, 2)
            wu = ffn_w_up[l // 2].reshape(D_MODEL, nck, D_FF_EXPERT).transpose(1, 0, 2)
            wd = ffn_w_down[l // 2].reshape(nck, D_FF_EXPERT, D_MODEL)
        h = _ffn_call(h1, gates, wg.astype(BF16), wu.astype(BF16), wd.astype(BF16), lw['ln2_g'], lw['ln2_b'],
                      batch, seq_rows, moe, l == DEPTH - 1)
    return h.reshape(batch, seq, D_MODEL).astype(x.dtype)
```

```python
import functools

import jax
import jax.numpy as jnp
import numpy as np
from jax import lax
from jax.experimental import pallas as pl
from jax.experimental.pallas import tpu as pltpu

D_MODEL = 1024
DEPTH = 2
N_META = 16
MLA_HEADS = 8
MLA_Q_RANK = 256
MLA_KV_RANK = 128
MLA_NOPE = 64
MLA_ROPE = 32
MLA_V = 64
ROPE_THETA = 10000.0
DN_HEADS = 4
DN_DK = 128
DN_DV = 128
DN_CONV = 4
CV_CH = 512
CV_WIDTH = 31
D_FF = 2816
N_EXPERTS = 8
D_FF_EXPERT = 1408
ALPHA = (2 * DEPTH) ** 0.25
EPS = 1e-6
NEG_INF = -1e30
MLA_QK = MLA_NOPE + MLA_ROPE
DN_QKV = 2 * DN_HEADS * DN_DK + DN_HEADS * DN_DV

LANES = 128
SUBLANES = 8
FRONT_PAD = 496
ROW_TILE = 512
ATT_HEADS_PER_STEP = 4
DN_CHUNK = 128
CV_HALO = 32
DN_HALO = 8
MISC_B = 64
MISC_A = 68
VMEM_LIMIT = 56 * 1024 * 1024

BF16 = jnp.bfloat16
F32 = jnp.float32


def _dot(a, b):
    return jnp.dot(a, b, preferred_element_type=F32)


def _dot_nt(a, b):
    return lax.dot_general(a, b, (((1,), (1,)), ((), ())), preferred_element_type=F32)


def _dot_tn(a, b):
    return lax.dot_general(a, b, (((0,), (0,)), ((), ())), preferred_element_type=F32)


def _rms(x, g):
    return x * lax.rsqrt(jnp.mean(x * x, axis=-1, keepdims=True) + EPS) * g


def _layer_norm(x, g, b):
    mu = jnp.mean(x, axis=-1, keepdims=True)
    xc = x - mu
    var = jnp.mean(xc * xc, axis=-1, keepdims=True)
    return xc * lax.rsqrt(var + EPS) * g + b


def _silu(x):
    return x * jax.nn.sigmoid(x)


def _const_spec(shape):
    return pl.BlockSpec(shape, lambda *_: (0,) * len(shape))


def _frames_spec(tile, tpb):
    return pl.BlockSpec((tile, D_MODEL), lambda i, *_: ((i // tpb) * (tpb - 1) + jnp.maximum(i % tpb - 1, 0), 0))


def _load_rows(h_ref, head_ref, tiles_per_batch):
    if head_ref is None:
        return h_ref[...]
    return jnp.where(pl.program_id(0) % tiles_per_batch == 0, head_ref[...], h_ref[...])


def _proj_kernel(*refs, tiles_per_batch, from_frames):
    h_ref, head_ref = (refs[0], refs[1]) if from_frames else (refs[0], None)
    (wa_ref, wq_ref, wk_ref, wv_ref, wb_ref, wc_ref, wd_ref, qg_ref, kvg_ref, vone_ref, ct_ref, st_ref, kc_ref,
     q_ref, k_ref, v_ref, dn_ref, z_ref, glu_ref, misc_ref) = refs[2 if from_frames else 1:]
    x = _load_rows(h_ref, head_ref, tiles_per_batch).astype(BF16)
    pa = _dot(x, wa_ref[...])
    cq = pa[:, :MLA_Q_RANK]
    ckv = pa[:, MLA_Q_RANK:MLA_Q_RANK + MLA_KV_RANK]
    blk = pa[:, MLA_Q_RANK + MLA_KV_RANK:]
    cqn = _rms(cq, qg_ref[...])
    qq = _dot(cqn.astype(BF16), wq_ref[...])
    ct = ct_ref[...]
    st = st_ref[...]
    hw = MLA_HEADS * LANES
    for h in range(MLA_HEADS):
        sl = slice(h * LANES, (h + 1) * LANES)
        sp = slice(hw + h * LANES, hw + (h + 1) * LANES)
        q_ref[:, sl] = (qq[:, sl] * ct + qq[:, sp] * st).astype(BF16)
    ckvn = _rms(ckv, kvg_ref[...])
    prod = blk * kc_ref[...]
    krr = prod + pltpu.roll(prod, LANES - MLA_ROPE, axis=1)
    kin = jnp.concatenate([ckvn, krr], axis=1).astype(BF16)
    k_ref[...] = _dot(kin, wk_ref[...]).astype(BF16)
    v_ref[...] = (_dot(ckvn.astype(BF16), wv_ref[...]) + vone_ref[...]).astype(BF16)
    dn_ref[...] = _dot(x, wb_ref[...])
    z_ref[...] = _dot(x, wc_ref[...])
    cv = _dot(x, wd_ref[...])
    glu_ref[...] = cv[:, :CV_CH] * jax.nn.sigmoid(cv[:, CV_CH:])
    misc_ref[...] = blk


def _proj_call(h, head, lw, tabs, tp, batch):
    tm = ROW_TILE
    tpb = tp // batch // tm
    row = lambda w: pl.BlockSpec((tm, w), lambda i: (i, 0))
    consts = [lw['wa'], lw['wq'], lw['wk'], lw['wv'], lw['wb'], lw['wc'], lw['wd'], lw['qg'], lw['kvg'], lw['vone']]
    if head is None:
        ins, in_specs = [h], [row(D_MODEL)]
    else:
        ins, in_specs = [h, head], [_frames_spec(tm, tpb), _const_spec(head.shape)]
    ins += consts + [tabs['ct'], tabs['st'], tabs['kc']]
    in_specs += [_const_spec(a.shape) for a in consts] + [row(LANES)] * 3
    out_shape = [
        jax.ShapeDtypeStruct((tp, MLA_HEADS * LANES), BF16),
        jax.ShapeDtypeStruct((tp, MLA_HEADS * LANES), BF16),
        jax.ShapeDtypeStruct((tp, MLA_HEADS * LANES), BF16),
        jax.ShapeDtypeStruct((tp, DN_QKV), F32),
        jax.ShapeDtypeStruct((tp, DN_HEADS * DN_DV), F32),
        jax.ShapeDtypeStruct((tp, CV_CH), F32),
        jax.ShapeDtypeStruct((tp, LANES), F32),
    ]
    out_specs = [row(s.shape[1]) for s in out_shape]
    return pl.pallas_call(
        functools.partial(_proj_kernel, tiles_per_batch=tpb, from_frames=head is not None),
        grid=(tp // tm,),
        in_specs=in_specs,
        out_specs=out_specs,
        out_shape=out_shape,
        compiler_params=pltpu.CompilerParams(dimension_semantics=("parallel",), vmem_limit_bytes=VMEM_LIMIT),
        name="proj",
    )(*ins)


def _attn_kernel(it_ref, jt_ref, q_ref, ka_ref, va_ref, kb_ref, vb_ref, *rest, tile, rq):
    o_ref, m_scr, acc_scr = rest[-3:]
    step = pl.program_id(2)
    i = it_ref[step]
    last = i + rq - 1
    ja = jt_ref[step]
    rows = rq * tile

    @pl.when(ja == 0)
    def _():
        m_scr[...] = jnp.full(m_scr.shape, NEG_INF, F32)
        acc_scr[...] = jnp.zeros(acc_scr.shape, F32)

    def body(r0, r1, tiles):
        keeps = []
        for _, _, jb, masked in tiles:
            if masked:
                qpos = i * tile + r0 + lax.broadcasted_iota(jnp.int32, (r1 - r0, tile), 0)
                kpos = jb * tile + lax.broadcasted_iota(jnp.int32, (r1 - r0, tile), 1)
                keeps.append((kpos <= qpos) & (kpos >= FRONT_PAD))
            else:
                keeps.append(None)
        for h in range(ATT_HEADS_PER_STEP):
            sl = slice(h * LANES, (h + 1) * LANES)
            q = q_ref[r0:r1, sl]
            ss = []
            for (kr, _, _, _), keep in zip(tiles, keeps):
                s = _dot_nt(q, kr[:, sl])
                ss.append(s if keep is None else jnp.where(keep, s, NEG_INF))
            m_prev = m_scr[h, r0:r1]
            m_cur = jnp.max(ss[0], axis=1, keepdims=True)
            for s in ss[1:]:
                m_cur = jnp.maximum(m_cur, jnp.max(s, axis=1, keepdims=True))
            m_new = jnp.maximum(m_prev, m_cur)
            m_scr[h, r0:r1] = m_new
            pv = None
            for (_, vr, _, _), s in zip(tiles, ss):
                t = _dot(jnp.exp2(s - m_new[:, :1]).astype(BF16), vr[:, sl])
                pv = t if pv is None else pv + t
            acc_scr[h, r0:r1] = jnp.exp2(m_prev - m_new) * acc_scr[h, r0:r1] + pv

    ta = (ka_ref, va_ref, ja)
    tb = (kb_ref, vb_ref, ja + 1)
    if rq == 1:
        two = ja + 1 <= i
        edge = (ja == 0) | (ja + 1 >= i)
        pl.when(two & jnp.logical_not(edge))(lambda: body(0, rows, [ta + (False,), tb + (False,)]))
        pl.when(two & edge)(lambda: body(0, rows, [ta + (True,), tb + (True,)]))
        pl.when(jnp.logical_not(two))(lambda: body(0, rows, [ta + (True,)]))
    else:
        diag = ja == i
        pl.when(jnp.logical_not(diag) & (ja > 0))(lambda: body(0, rows, [ta + (False,), tb + (False,)]))
        pl.when(jnp.logical_not(diag) & (ja == 0))(lambda: body(0, rows, [ta + (True,), tb + (False,)]))

        @pl.when(diag)
        def _():
            body(0, tile, [ta + (True,)])
            body(tile, rows, [ta + (True,), tb + (True,)])

    @pl.when(ja + 1 >= last)
    def _():
        lane = lax.broadcasted_iota(jnp.int32, (rows, LANES), 1)
        for hp in range(ATT_HEADS_PER_STEP // 2):
            a0 = acc_scr[2 * hp]
            a1 = acc_scr[2 * hp + 1]
            lo = a0 / a0[:, MLA_V:MLA_V + 1]
            hi = pltpu.roll(a1 / a1[:, MLA_V:MLA_V + 1], MLA_V, axis=1)
            o_ref[:, hp * LANES:(hp + 1) * LANES] = jnp.where(lane < MLA_V, lo, hi).astype(BF16)


def _attn_call(q, k, v, batch):
    tp = q.shape[0]
    lp = tp // batch
    tile = ROW_TILE
    nb = lp // tile
    qw = MLA_HEADS * LANES
    ow = MLA_HEADS * MLA_V
    q3, k3, v3 = (a.reshape(batch, lp, qw) for a in (q, k, v))

    def call(rq, q_tiles, prev):
        steps = [(i, ja) for i in q_tiles for ja in range(0, i + rq, 2)]
        it = jnp.asarray(np.array([p[0] for p in steps], np.int32))
        jt = jnp.asarray(np.array([p[1] for p in steps], np.int32))
        qmap = lambda b, g, s, it, jt: (b, it[s] // rq, g)
        first = lambda b, g, s, it, jt: (b, jt[s], g)
        second = lambda b, g, s, it, jt: (b, jnp.minimum(jt[s] + 1, it[s] + rq - 1), g)
        gw = ATT_HEADS_PER_STEP * LANES
        in_specs = [
            pl.BlockSpec((None, rq * tile, gw), qmap),
            pl.BlockSpec((None, tile, gw), first),
            pl.BlockSpec((None, tile, gw), first),
            pl.BlockSpec((None, tile, gw), second),
            pl.BlockSpec((None, tile, gw), second),
        ]
        args = [it, jt, q3, k3, v3, k3, v3]
        aliases = {}
        if prev is not None:
            in_specs.append(pl.BlockSpec(memory_space=pl.ANY))
            args.append(prev)
            aliases = {len(args) - 1: 0}
        grid_spec = pltpu.PrefetchScalarGridSpec(
            num_scalar_prefetch=2,
            grid=(batch, MLA_HEADS // ATT_HEADS_PER_STEP, len(steps)),
            in_specs=in_specs,
            out_specs=pl.BlockSpec((None, rq * tile, ATT_HEADS_PER_STEP * MLA_V), qmap),
            scratch_shapes=[
                pltpu.VMEM((ATT_HEADS_PER_STEP, rq * tile, LANES), F32),
                pltpu.VMEM((ATT_HEADS_PER_STEP, rq * tile, LANES), F32),
            ],
        )
        return pl.pallas_call(
            functools.partial(_attn_kernel, tile=tile, rq=rq),
            grid_spec=grid_spec,
            out_shape=jax.ShapeDtypeStruct((batch, lp, ow), BF16),
            input_output_aliases=aliases,
            compiler_params=pltpu.CompilerParams(dimension_semantics=("parallel", "parallel", "arbitrary"),
                                                 vmem_limit_bytes=VMEM_LIMIT),
            name="mla_attn_x%d" % rq,
        )(*args)

    out = None
    if nb >= 2:
        out = call(2, list(range(0, nb - nb % 2, 2)), None)
    if nb % 2:
        out = call(1, [nb - 1], out)
    return out.reshape(tp, ow)


def _dn_kernel(x_ref, halo_ref, z_ref, misc_ref, cw_ref, par_ref, on_ref, o_ref, buf, s_scr, *, tile, tiles_per_batch):
    t = pl.program_id(0)
    tb = t % tiles_per_batch
    first = tb == 0

    @pl.when(first)
    def _():
        s_scr[...] = jnp.zeros(s_scr.shape, F32)

    buf[0:DN_HALO, :] = jnp.where(first, 0.0, halo_ref[...])
    buf[DN_HALO:DN_HALO + tile, :] = x_ref[...]
    y = cw_ref[0:1, :] * buf[pl.ds(DN_HALO - DN_CONV + 1, tile), :]
    for jj in range(1, DN_CONV):
        y = y + cw_ref[jj:jj + 1, :] * buf[pl.ds(DN_HALO - DN_CONV + 1 + jj, tile), :]
    y = _silu(y)

    misc = misc_ref[...]
    row = tb * tile + lax.broadcasted_iota(jnp.int32, (tile, LANES), 0)
    valid = row >= FRONT_PAD
    beta_all = jnp.where(valid, jax.nn.sigmoid(misc), 0.0)
    g_all = jnp.where(valid, -jnp.exp(par_ref[0:1, :]) * jax.nn.softplus(misc + par_ref[1:2, :]), 0.0)
    rin = lax.broadcasted_iota(jnp.int32, (tile, LANES), 0) % DN_CHUNK
    gc_all = g_all
    sh = 1
    while sh < DN_CHUNK:
        gc_all = gc_all + jnp.where(rin >= sh, pltpu.roll(gc_all, sh, axis=0), 0.0)
        sh *= 2
    gct_all = gc_all.T

    ii = lax.broadcasted_iota(jnp.int32, (DN_CHUNK, DN_CHUNK), 0)
    jj_ = lax.broadcasted_iota(jnp.int32, (DN_CHUNK, DN_CHUNK), 1)
    incl = ii >= jj_
    strict = ii > jj_
    eye = jnp.where(ii == jj_, 1.0, 0.0).astype(F32)
    onorm = on_ref[...]

    nch = tile // DN_CHUNK
    items = []
    for h in range(DN_HEADS):
        hs = slice(h * DN_DK, (h + 1) * DN_DK)
        qh = y[:, hs]
        kh = y[:, DN_HEADS * DN_DK + h * DN_DK:DN_HEADS * DN_DK + (h + 1) * DN_DK]
        vh = y[:, 2 * DN_HEADS * DN_DK + h * DN_DV:2 * DN_HEADS * DN_DK + (h + 1) * DN_DV]
        qh = qh * lax.rsqrt(jnp.sum(qh * qh, axis=-1, keepdims=True) + EPS) * (DN_DK ** -0.5)
        kh = kh * lax.rsqrt(jnp.sum(kh * kh, axis=-1, keepdims=True) + EPS)
        for c in range(nch):
            rs = slice(c * DN_CHUNK, (c + 1) * DN_CHUNK)
            gcol = gc_all[rs, MISC_A + h:MISC_A + h + 1]
            grow = gct_all[MISC_A + h:MISC_A + h + 1, rs]
            bcol = beta_all[rs, MISC_B + h:MISC_B + h + 1]
            gamma = jnp.where(incl, jnp.exp(jnp.where(incl, gcol - grow, 0.0)), 0.0)
            qc = qh[rs]
            kc = kh[rs]
            kb = kc * bcol
            eg = jnp.exp(gcol)
            glast = gcol[DN_CHUNK - 1:DN_CHUNK, :]
            items.append(dict(
                h=h, rs=rs,
                a=jnp.where(strict, _dot_nt(kb, kc) * gamma, 0.0),
                rhs=jnp.concatenate([vh[rs] * bcol, kb * eg], axis=1),
                qk=jnp.where(incl, _dot_nt(qc, kc) * gamma, 0.0),
                q_dec=qc * eg,
                k_dec=kc * jnp.exp(glast - gcol),
                cd=jnp.exp(glast),
            ))
    xs = [-it['a'] for it in items]
    ts = [eye + x for x in xs]
    for _ in range(6):
        xs = [_dot(x, x) for x in xs]
        ts = [t + _dot(t, x) for t, x in zip(ts, xs)]
    for it, tinv in zip(items, ts):
        sol = _dot(tinv, it['rhs'])
        qs = _dot(it['qk'], sol)
        ks = _dot_tn(it['k_dec'], sol)
        it['ob'] = qs[:, :DN_DV]
        it['q_eff'] = it['q_dec'] - qs[:, DN_DV:]
        it['s_add'] = ks[:, :DN_DV]
        it['s_mul'] = ks[:, DN_DV:]
    states = [s_scr[h] for h in range(DN_HEADS)]
    for c in range(nch):
        for h in range(DN_HEADS):
            it = items[h * nch + c]
            st = states[h]
            o = _dot(it['q_eff'], st) + it['ob']
            states[h] = it['cd'] * st + it['s_add'] - _dot(it['s_mul'], st)
            on = _rms(o, onorm)
            zz = z_ref[it['rs'], h * DN_DV:(h + 1) * DN_DV]
            o_ref[it['rs'], h * DN_DV:(h + 1) * DN_DV] = (on * _silu(zz)).astype(BF16)
    for h in range(DN_HEADS):
        s_scr[h] = states[h]


def _dn_call(dn_pre, z, misc, lw, batch):
    tp = dn_pre.shape[0]
    tile = ROW_TILE
    tpb = tp // batch // tile
    hb = tile // DN_HALO
    return pl.pallas_call(
        functools.partial(_dn_kernel, tile=tile, tiles_per_batch=tpb),
        grid=(tp // tile,),
        in_specs=[
            pl.BlockSpec((tile, DN_QKV), lambda t: (t, 0)),
            pl.BlockSpec((DN_HALO, DN_QKV), lambda t: (jnp.maximum(t * hb - 1, 0), 0)),
            pl.BlockSpec((tile, DN_HEADS * DN_DV), lambda t: (t, 0)),
            pl.BlockSpec((tile, LANES), lambda t: (t, 0)),
            _const_spec(lw['dn_cw'].shape),
            _const_spec(lw['dn_par'].shape),
            _const_spec(lw['dn_on'].shape),
        ],
        out_specs=pl.BlockSpec((tile, DN_HEADS * DN_DV), lambda t: (t, 0)),
        out_shape=jax.ShapeDtypeStruct((tp, DN_HEADS * DN_DV), BF16),
        scratch_shapes=[
            pltpu.VMEM((DN_HALO + tile, DN_QKV), F32),
            pltpu.VMEM((DN_HEADS, DN_DK, DN_DV), F32),
        ],
        compiler_params=pltpu.CompilerParams(dimension_semantics=("arbitrary",), vmem_limit_bytes=VMEM_LIMIT),
        name="deltanet",
    )(dn_pre, dn_pre, z, misc, lw['dn_cw'], lw['dn_par'], lw['dn_on'])


def _cv_kernel(x_ref, halo_ref, w_ref, b_ref, g_ref, be_ref, o_ref, buf, *, tile, tiles_per_batch):
    t = pl.program_id(0)
    first = (t % tiles_per_batch) == 0
    buf[0, 0:CV_HALO, :] = jnp.where(first, 0.0, halo_ref[...])
    buf[0, CV_HALO:CV_HALO + tile, :] = x_ref[...]
    span = CV_HALO + tile - SUBLANES
    for r in range(1, SUBLANES):
        buf[r, 0:span, :] = buf[0, pl.ds(r, span), :]
    base = CV_HALO - CV_WIDTH + 1
    acc = None
    for j in range(CV_WIDTH):
        r = (base + j) % SUBLANES
        term = w_ref[j:j + 1, :] * buf[r, pl.ds(base + j - r, tile), :]
        acc = term if acc is None else acc + term
    acc = acc + b_ref[...]
    o_ref[...] = _silu(_layer_norm(acc, g_ref[...], be_ref[...])).astype(BF16)


def _cv_call(glu, lw, batch):
    tp = glu.shape[0]
    tile = ROW_TILE
    tpb = tp // batch // tile
    hb = tile // CV_HALO
    return pl.pallas_call(
        functools.partial(_cv_kernel, tile=tile, tiles_per_batch=tpb),
        grid=(tp // tile,),
        in_specs=[
            pl.BlockSpec((tile, CV_CH), lambda t: (t, 0)),
            pl.BlockSpec((CV_HALO, CV_CH), lambda t: (jnp.maximum(t * hb - 1, 0), 0)),
            _const_spec(lw['cv_w'].shape),
            _const_spec(lw['cv_b'].shape),
            _const_spec(lw['cv_g'].shape),
            _const_spec(lw['cv_be'].shape),
        ],
        out_specs=pl.BlockSpec((tile, CV_CH), lambda t: (t, 0)),
        out_shape=jax.ShapeDtypeStruct((tp, CV_CH), BF16),
        scratch_shapes=[pltpu.VMEM((SUBLANES, CV_HALO + tile, CV_CH), F32)],
        compiler_params=pltpu.CompilerParams(dimension_semantics=("parallel",), vmem_limit_bytes=VMEM_LIMIT),
        name="cv_conv",
    )(glu, glu, lw['cv_w'], lw['cv_b'], lw['cv_g'], lw['cv_be'])


def _merge_kernel(*refs, tile, tiles_per_batch, seq_rows, route, from_frames):
    h_ref, head_ref = (refs[0], refs[1]) if from_frames else (refs[0], None)
    (om_ref, od_ref, oc_ref, wg_ref, wbm_ref, wbd_ref, wbc_ref, wmo_ref, g_ref, b_ref, wrh_ref, wrl_ref,
     o_ref, gate_ref) = refs[2 if from_frames else 1:]
    hf = _load_rows(h_ref, head_ref, tiles_per_batch)
    x = hf.astype(BF16)
    merged = None
    for br, (src, wbr) in enumerate(((om_ref, wbm_ref), (od_ref, wbd_ref), (oc_ref, wbc_ref))):
        gl = _dot(x, wg_ref[:, br * D_MODEL:(br + 1) * D_MODEL])
        yb = _dot(src[...], wbr[...])
        term = jax.nn.sigmoid(gl) * yb
        merged = term if merged is None else merged + term
    mix = _dot(merged.astype(BF16), wmo_ref[...])
    h1 = _layer_norm(ALPHA * hf + mix, g_ref[...], b_ref[...])
    row = (pl.program_id(0) % tiles_per_batch) * tile + lax.broadcasted_iota(jnp.int32, (tile, 1), 0)
    live = (row >= FRONT_PAD) & (row < seq_rows)
    h1 = jnp.where(live, h1, 0.0)
    o_ref[...] = h1
    if route:
        h1_hi = h1.astype(BF16)
        h1_lo = (h1 - h1_hi.astype(F32)).astype(BF16)
        hh = _dot(h1_hi, wrl_ref[...])
        logits = hh[:, :LANES] + hh[:, LANES:] + _dot(h1_lo, wrh_ref[...])
        lane = lax.broadcasted_iota(jnp.int32, (tile, LANES), 1)
        logits = jnp.where(lane < N_EXPERTS, logits, -jnp.inf)
        m1 = jnp.max(logits, axis=1, keepdims=True)
        i1 = jnp.min(jnp.where(logits == m1, lane, LANES), axis=1, keepdims=True)
        rest = jnp.where(lane == i1, -jnp.inf, logits)
        m2 = jnp.max(rest, axis=1, keepdims=True)
        i2 = jnp.min(jnp.where(rest == m2, lane, LANES), axis=1, keepdims=True)
        e2 = jnp.exp(m2 - m1)
        den = 1.0 + e2
        gate_ref[...] = jnp.where(lane == i1, 1.0 / den, jnp.where(lane == i2, e2 / den, 0.0))
    else:
        gate_ref[...] = jnp.ones(gate_ref.shape, F32)


def _merge_call(h, head, o_mla, o_dn, o_cv, lw, batch, seq_rows, route):
    tp = o_mla.shape[0]
    tile = ROW_TILE
    tpb = tp // batch // tile
    row = lambda w: pl.BlockSpec((tile, w), lambda i: (i, 0))
    consts = [lw['wg'], lw['wbm'], lw['wbd'], lw['wbc'], lw['wmo'], lw['ln1_g'], lw['ln1_b'], lw['wr_hi'], lw['wr_lo']]
    if head is None:
        ins, in_specs = [h], [row(D_MODEL)]
    else:
        ins, in_specs = [h, head], [_frames_spec(tile, tpb), _const_spec(head.shape)]
    return pl.pallas_call(
        functools.partial(_merge_kernel, tile=tile, tiles_per_batch=tpb, seq_rows=seq_rows, route=route,
                          from_frames=head is not None),
        grid=(tp // tile,),
        in_specs=in_specs + [row(512), row(512), row(512)] + [_const_spec(a.shape) for a in consts],
        out_specs=[row(D_MODEL), row(LANES)],
        out_shape=[jax.ShapeDtypeStruct((tp, D_MODEL), F32), jax.ShapeDtypeStruct((tp, LANES), F32)],
        compiler_params=pltpu.CompilerParams(dimension_semantics=("parallel",), vmem_limit_bytes=VMEM_LIMIT),
        name="merge_ln1",
    )(*ins, o_mla, o_dn, o_cv, *consts)


def _ffn_kernel(h_ref, gate_ref, wg_ref, wu_ref, wd_ref, g_ref, b_ref, o_ref, acc, *, tile, tiles_per_batch, seq_rows, gated):
    c = pl.program_id(1)
    x = h_ref[...].astype(BF16)
    hid = _silu(_dot(x, wg_ref[0])) * _dot(x, wu_ref[0])
    y = _dot(hid.astype(BF16), wd_ref[0])
    if gated:
        lane = lax.broadcasted_iota(jnp.int32, (tile, LANES), 1)
        y = y * jnp.sum(jnp.where(lane == c, gate_ref[...], 0.0), axis=1, keepdims=True)

    @pl.when(c == 0)
    def _():
        acc[...] = y

    @pl.when(c > 0)
    def _():
        acc[...] = acc[...] + y

    @pl.when(c == pl.num_programs(1) - 1)
    def _():
        h2 = _layer_norm(ALPHA * h_ref[...] + acc[...], g_ref[...], b_ref[...])
        row = (pl.program_id(0) % tiles_per_batch) * tile + lax.broadcasted_iota(jnp.int32, (tile, 1), 0)
        live = (row >= FRONT_PAD) & (row < seq_rows)
        o_ref[...] = jnp.where(live, h2, 0.0)


def _ffn_call(h, gates, wg, wu, wd, ln_g, ln_b, batch, seq_rows, gated, to_frames):
    tp = h.shape[0]
    tile = ROW_TILE
    tpb = tp // batch // tile
    nchunk, _, width = wg.shape
    if to_frames:
        out_spec = _frames_spec(tile, tpb)
        out_rows = tp - batch * tile
        semantics = ("arbitrary", "arbitrary")
    else:
        out_spec = pl.BlockSpec((tile, D_MODEL), lambda i, c: (i, 0))
        out_rows = tp
        semantics = ("parallel", "arbitrary")
    return pl.pallas_call(
        functools.partial(_ffn_kernel, tile=tile, tiles_per_batch=tpb, seq_rows=seq_rows, gated=gated),
        grid=(tp // tile, nchunk),
        in_specs=[
            pl.BlockSpec((tile, D_MODEL), lambda i, c: (i, 0)),
            pl.BlockSpec((tile, LANES), lambda i, c: (i, 0)),
            pl.BlockSpec((1, D_MODEL, width), lambda i, c: (c, 0, 0)),
            pl.BlockSpec((1, D_MODEL, width), lambda i, c: (c, 0, 0)),
            pl.BlockSpec((1, width, D_MODEL), lambda i, c: (c, 0, 0)),
            _const_spec(ln_g.shape),
            _const_spec(ln_b.shape),
        ],
        out_specs=out_spec,
        out_shape=jax.ShapeDtypeStruct((out_rows, D_MODEL), F32),
        scratch_shapes=[pltpu.VMEM((tile, D_MODEL), F32)],
        compiler_params=pltpu.CompilerParams(dimension_semantics=semantics, vmem_limit_bytes=VMEM_LIMIT),
        name="ffn_ln2",
    )(h, gates, wg, wu, wd, ln_g, ln_b)


def _row(v, width=None):
    v = v.astype(F32).reshape(1, -1)
    if width is not None and v.shape[1] < width:
        v = jnp.pad(v, ((0, 0), (0, width - v.shape[1])))
    return v


def _layer_weights(l, w_in, mla_q_norm, mla_w_uq, mla_kv_norm, mla_w_ukv, dn_conv_w, dn_a_log, dn_dt_bias, dn_o_norm,
                   cv_dw_w, cv_dw_b, cv_ln_g, cv_ln_b, w_br_mla, w_br_dn, w_br_cv, w_mix_out, ln1_g, ln1_b, ln2_g, ln2_b,
                   moe_w_router):
    sizes = (MLA_Q_RANK, MLA_KV_RANK, MLA_ROPE, DN_QKV, DN_HEADS * DN_DV, DN_HEADS, DN_HEADS, CV_CH, CV_CH,
             D_MODEL, D_MODEL, D_MODEL)
    offs = np.cumsum((0,) + sizes)
    col = lambda i: w_in[l][:, offs[i]:offs[i + 1]]
    half = MLA_ROPE // 2
    w_kr = col(2)
    w_kr_pair = jnp.concatenate([-w_kr[:, half:], w_kr[:, :half]], axis=1)
    wa = jnp.concatenate([col(0), col(1), w_kr, w_kr_pair, col(5), col(6)], axis=1)
    wa = jnp.pad(wa, ((0, 0), (0, 512 - wa.shape[1])))
    wuq = mla_w_uq[l].reshape(MLA_Q_RANK, MLA_HEADS, MLA_QK)
    zq = jnp.zeros((MLA_Q_RANK, MLA_HEADS, LANES - MLA_QK), F32)
    wq_main = jnp.concatenate([wuq, zq], axis=2)
    wq_pair = jnp.concatenate([jnp.zeros((MLA_Q_RANK, MLA_HEADS, MLA_NOPE), F32), -wuq[:, :, MLA_NOPE + half:],
                               wuq[:, :, MLA_NOPE:MLA_NOPE + half], zq], axis=2)
    wq = jnp.concatenate([wq_main.reshape(MLA_Q_RANK, -1), wq_pair.reshape(MLA_Q_RANK, -1)], axis=1)
    wukv = mla_w_ukv[l].reshape(MLA_KV_RANK, MLA_HEADS, MLA_NOPE + MLA_V)
    wk_lat = jnp.concatenate([wukv[:, :, :MLA_NOPE], jnp.zeros((MLA_KV_RANK, MLA_HEADS, LANES - MLA_NOPE), F32)], axis=2)
    place = np.zeros((LANES, MLA_HEADS, LANES), np.float32)
    for r in range(MLA_ROPE):
        place[r, :, MLA_NOPE + r] = 1.0
    wk = jnp.concatenate([wk_lat.reshape(MLA_KV_RANK, -1), jnp.asarray(place).reshape(LANES, -1)], axis=0)
    wv = jnp.concatenate([wukv[:, :, MLA_NOPE:], jnp.zeros((MLA_KV_RANK, MLA_HEADS, LANES - MLA_V), F32)],
                         axis=2).reshape(MLA_KV_RANK, -1)
    vone = np.zeros((1, MLA_HEADS, LANES), np.float32)
    vone[:, :, MLA_V] = 1.0
    par = jnp.zeros((8, LANES), F32)
    par = par.at[0, MISC_A:MISC_A + DN_HEADS].set(dn_a_log[l].astype(F32))
    par = par.at[1, MISC_A:MISC_A + DN_HEADS].set(dn_dt_bias[l].astype(F32))
    if moe_w_router is None:
        wr = jnp.zeros((D_MODEL, LANES), F32)
    else:
        wr = jnp.pad(moe_w_router.astype(F32), ((0, 0), (0, LANES - N_EXPERTS)))
    wr_hi = wr.astype(BF16)
    wr_lo = jnp.concatenate([wr_hi, (wr - wr_hi.astype(F32)).astype(BF16)], axis=1)
    return dict(
        vone=jnp.asarray(vone.reshape(1, -1)), wr_hi=wr_hi, wr_lo=wr_lo,
        wa=wa.astype(BF16), wq=wq.astype(BF16), wk=wk.astype(BF16), wv=wv.astype(BF16),
        wb=col(3).astype(BF16), wc=col(4).astype(BF16), wd=jnp.concatenate([col(7), col(8)], axis=1).astype(BF16),
        qg=_row(mla_q_norm[l]), kvg=_row(mla_kv_norm[l]),
        dn_cw=jnp.pad(dn_conv_w[l].astype(F32), ((0, 8 - DN_CONV), (0, 0))), dn_par=par, dn_on=_row(dn_o_norm[l]),
        cv_w=jnp.pad(cv_dw_w[l].astype(F32), ((0, 32 - CV_WIDTH), (0, 0))), cv_b=_row(cv_dw_b[l]),
        cv_g=_row(cv_ln_g[l]), cv_be=_row(cv_ln_b[l]),
        wg=jnp.concatenate([col(9), col(10), col(11)], axis=1).astype(BF16),
        wbm=w_br_mla[l].astype(BF16), wbd=w_br_dn[l].astype(BF16), wbc=w_br_cv[l].astype(BF16),
        wmo=w_mix_out[l].astype(BF16), ln1_g=_row(ln1_g[l]), ln1_b=_row(ln1_b[l]),
        ln2_g=_row(ln2_g[l]), ln2_b=_row(ln2_b[l]),
    )


def _rope_tables(lp):
    pos = jnp.maximum(jnp.arange(lp, dtype=jnp.int32) - FRONT_PAD, 0).astype(F32)
    inv_freq = ROPE_THETA ** (-jnp.arange(0, MLA_ROPE, 2, dtype=F32) / MLA_ROPE)
    ang = pos[:, None] * inv_freq[None, :]
    cos2 = jnp.concatenate([jnp.cos(ang), jnp.cos(ang)], axis=1)
    sin2 = jnp.concatenate([jnp.sin(ang), jnp.sin(ang)], axis=1)
    scale = MLA_QK ** -0.5 * np.log2(np.e)
    zpad = jnp.zeros((lp, LANES - MLA_QK), F32)
    ct = jnp.concatenate([jnp.full((lp, MLA_NOPE), scale, F32), scale * cos2, zpad], axis=1)
    st = jnp.concatenate([jnp.zeros((lp, MLA_NOPE), F32), scale * sin2, zpad], axis=1)
    kc = jnp.concatenate([cos2, sin2, jnp.zeros((lp, LANES - 2 * MLA_ROPE), F32)], axis=1)
    return ct, st, kc


def kernel(x, meta_tokens, w_in, mla_q_norm, mla_w_uq, mla_kv_norm, mla_w_ukv, dn_conv_w, dn_a_log, dn_dt_bias, dn_o_norm, cv_dw_w, cv_dw_b, cv_ln_g, cv_ln_b, w_br_mla, w_br_dn, w_br_cv, w_mix_out, ln1_g, ln1_b, ln2_g, ln2_b, ffn_w_gate, ffn_w_up, ffn_w_down, moe_w_router, moe_w_gate, moe_w_up, moe_w_down):
    batch, seq, _ = x.shape
    assert FRONT_PAD + N_META == ROW_TILE and seq % ROW_TILE == 0 and seq >= ROW_TILE
    lp = seq_rows = ROW_TILE + seq
    tp = batch * lp
    head = jnp.concatenate([jnp.zeros((FRONT_PAD, D_MODEL), F32), meta_tokens.astype(F32)], axis=0)
    h = x.astype(F32).reshape(batch * seq, D_MODEL)
    ct, st, kc = _rope_tables(lp)
    tabs = dict(ct=jnp.tile(ct, (batch, 1)), st=jnp.tile(st, (batch, 1)), kc=jnp.tile(kc, (batch, 1)))
    for l in range(DEPTH):
        moe = l % 2 == 1
        lw = _layer_weights(l, w_in, mla_q_norm, mla_w_uq, mla_kv_norm, mla_w_ukv, dn_conv_w, dn_a_log, dn_dt_bias,
                            dn_o_norm, cv_dw_w, cv_dw_b, cv_ln_g, cv_ln_b, w_br_mla, w_br_dn, w_br_cv, w_mix_out,
                            ln1_g, ln1_b, ln2_g, ln2_b, moe_w_router[l // 2] if moe else None)
        src_head = head if l == 0 else None
        q, k, v, dn_pre, z, glu, misc = _proj_call(h, src_head, lw, tabs, tp, batch)
        o_mla = _attn_call(q, k, v, batch)
        o_dn = _dn_call(dn_pre, z, misc, lw, batch)
        o_cv = _cv_call(glu, lw, batch)
        h1, gates = _merge_call(h, src_head, o_mla, o_dn, o_cv, lw, batch, seq_rows, moe)
        if moe:
            wg, wu, wd = moe_w_gate[l // 2], moe_w_up[l // 2], moe_w_down[l // 2]
        else:
            nck = D_FF // D_FF_EXPERT
            wg = ffn_w_gate[l // 2].reshape(D_MODEL, nck, D_FF_EXPERT).transpose(1, 0, 2)
            wu = ffn_w_up[l // 2].reshape(D_MODEL, nck, D_FF_EXPERT).transpose(1, 0, 2)
            wd = ffn_w_down[l // 2].reshape(nck, D_FF_EXPERT, D_MODEL)
        h = _ffn_call(h1, gates, wg.astype(BF16), wu.astype(BF16), wd.astype(BF16), lw['ln2_g'], lw['ln2_b'],
                      batch, seq_rows, moe, l == DEPTH - 1)
    return h.reshape(batch, seq, D_MODEL).astype(x.dtype)
```

```python
import functools

import jax
import jax.numpy as jnp
import numpy as np
from jax import lax
from jax.experimental import pallas as pl
from jax.experimental.pallas import tpu as pltpu

D_MODEL = 1024
DEPTH = 2
N_META = 16
MLA_HEADS = 8
MLA_Q_RANK = 256
MLA_KV_RANK = 128
MLA_NOPE = 64
MLA_ROPE = 32
MLA_V = 64
ROPE_THETA = 10000.0
DN_HEADS = 4
DN_DK = 128
DN_DV = 128
DN_CONV = 4
CV_CH = 512
CV_WIDTH = 31
D_FF = 2816
N_EXPERTS = 8
D_FF_EXPERT = 1408
ALPHA = (2 * DEPTH) ** 0.25
EPS = 1e-6
NEG_INF = -1e30
MLA_QK = MLA_NOPE + MLA_ROPE
DN_QKV = 2 * DN_HEADS * DN_DK + DN_HEADS * DN_DV

LANES = 128
SUBLANES = 8
FRONT_PAD = 496
ROW_TILE = 512
ATT_TALL = 4
ATT_TALL_HEADS = 2
ATT_TAIL_HEADS = 4
DN_CHUNK = 128
CV_HALO = 32
DN_HALO = 8
MISC_B = 64
MISC_A = 68
VMEM_LIMIT = 56 * 1024 * 1024

BF16 = jnp.bfloat16
F32 = jnp.float32


def _dot(a, b):
    return jnp.dot(a, b, preferred_element_type=F32)


def _dot_nt(a, b):
    return lax.dot_general(a, b, (((1,), (1,)), ((), ())), preferred_element_type=F32)


def _dot_tn(a, b):
    return lax.dot_general(a, b, (((0,), (0,)), ((), ())), preferred_element_type=F32)


def _rms(x, g):
    return x * lax.rsqrt(jnp.mean(x * x, axis=-1, keepdims=True) + EPS) * g


def _layer_norm(x, g, b):
    mu = jnp.mean(x, axis=-1, keepdims=True)
    xc = x - mu
    var = jnp.mean(xc * xc, axis=-1, keepdims=True)
    return xc * lax.rsqrt(var + EPS) * g + b


def _silu(x):
    return x * jax.nn.sigmoid(x)


def _const_spec(shape):
    return pl.BlockSpec(shape, lambda *_: (0,) * len(shape))


def _frames_spec(tile, tpb):
    return pl.BlockSpec((tile, D_MODEL), lambda i, *_: ((i // tpb) * (tpb - 1) + jnp.maximum(i % tpb - 1, 0), 0))


def _load_rows(h_ref, head_ref, tiles_per_batch):
    if head_ref is None:
        return h_ref[...]
    return jnp.where(pl.program_id(0) % tiles_per_batch == 0, head_ref[...], h_ref[...])


def _proj_kernel(*refs, tiles_per_batch, from_frames):
    h_ref, head_ref = (refs[0], refs[1]) if from_frames else (refs[0], None)
    (wa_ref, wq_ref, wk_ref, wv_ref, wb_ref, wc_ref, wd_ref, qg_ref, kvg_ref, vone_ref, ct_ref, st_ref, kc_ref,
     q_ref, k_ref, v_ref, dn_ref, z_ref, glu_ref, misc_ref) = refs[2 if from_frames else 1:]
    x = _load_rows(h_ref, head_ref, tiles_per_batch).astype(BF16)
    pa = _dot(x, wa_ref[...])
    cq = pa[:, :MLA_Q_RANK]
    ckv = pa[:, MLA_Q_RANK:MLA_Q_RANK + MLA_KV_RANK]
    blk = pa[:, MLA_Q_RANK + MLA_KV_RANK:]
    cqn = _rms(cq, qg_ref[...])
    qq = _dot(cqn.astype(BF16), wq_ref[...])
    ct = ct_ref[...]
    st = st_ref[...]
    hw = MLA_HEADS * LANES
    qone = jnp.where(lax.broadcasted_iota(jnp.int32, (1, LANES), 1) == MLA_QK, 1.0, 0.0)
    for h in range(MLA_HEADS):
        sl = slice(h * LANES, (h + 1) * LANES)
        sp = slice(hw + h * LANES, hw + (h + 1) * LANES)
        q_ref[:, sl] = (qq[:, sl] * ct + qq[:, sp] * st + qone).astype(BF16)
    ckvn = _rms(ckv, kvg_ref[...])
    prod = blk * kc_ref[...]
    krr = prod + pltpu.roll(prod, LANES - MLA_ROPE, axis=1)
    kin = jnp.concatenate([ckvn, krr], axis=1).astype(BF16)
    kk = _dot(kin, wk_ref[...])
    krow = lax.broadcasted_iota(jnp.int32, kk.shape, 0)
    klane = lax.broadcasted_iota(jnp.int32, kk.shape, 1) % LANES
    padded = (pl.program_id(0) % tiles_per_batch == 0) & (krow < FRONT_PAD) & (klane == MLA_QK)
    k_ref[...] = jnp.where(padded, NEG_INF, kk).astype(BF16)
    v_ref[...] = (_dot(ckvn.astype(BF16), wv_ref[...]) + vone_ref[...]).astype(BF16)
    dn_ref[...] = _dot(x, wb_ref[...])
    z_ref[...] = _dot(x, wc_ref[...])
    cv = _dot(x, wd_ref[...])
    glu_ref[...] = cv[:, :CV_CH] * jax.nn.sigmoid(cv[:, CV_CH:])
    misc_ref[...] = blk


def _proj_call(h, head, lw, tabs, tp, batch):
    tm = ROW_TILE
    tpb = tp // batch // tm
    row = lambda w: pl.BlockSpec((tm, w), lambda i: (i, 0))
    consts = [lw['wa'], lw['wq'], lw['wk'], lw['wv'], lw['wb'], lw['wc'], lw['wd'], lw['qg'], lw['kvg'], lw['vone']]
    if head is None:
        ins, in_specs = [h], [row(D_MODEL)]
    else:
        ins, in_specs = [h, head], [_frames_spec(tm, tpb), _const_spec(head.shape)]
    ins += consts + [tabs['ct'], tabs['st'], tabs['kc']]
    in_specs += [_const_spec(a.shape) for a in consts] + [row(LANES)] * 3
    out_shape = [
        jax.ShapeDtypeStruct((tp, MLA_HEADS * LANES), BF16),
        jax.ShapeDtypeStruct((tp, MLA_HEADS * LANES), BF16),
        jax.ShapeDtypeStruct((tp, MLA_HEADS * LANES), BF16),
        jax.ShapeDtypeStruct((tp, DN_QKV), F32),
        jax.ShapeDtypeStruct((tp, DN_HEADS * DN_DV), F32),
        jax.ShapeDtypeStruct((tp, CV_CH), F32),
        jax.ShapeDtypeStruct((tp, LANES), F32),
    ]
    out_specs = [row(s.shape[1]) for s in out_shape]
    return pl.pallas_call(
        functools.partial(_proj_kernel, tiles_per_batch=tpb, from_frames=head is not None),
        grid=(tp // tm,),
        in_specs=in_specs,
        out_specs=out_specs,
        out_shape=out_shape,
        compiler_params=pltpu.CompilerParams(dimension_semantics=("parallel",), vmem_limit_bytes=VMEM_LIMIT),
        name="proj",
    )(*ins)


def _attn_kernel(it_ref, jt_ref, q_ref, ka_ref, va_ref, kb_ref, vb_ref, *rest, tile, rq, hps):
    o_ref, m_scr, acc_scr = rest[-3:]
    step = pl.program_id(2)
    i = it_ref[step]
    last = i + rq - 1
    ja = jt_ref[step]
    rows = rq * tile

    @pl.when(ja == 0)
    def _():
        m_scr[...] = jnp.full(m_scr.shape, NEG_INF, F32)
        acc_scr[...] = jnp.zeros(acc_scr.shape, F32)

    def body(r0, r1, tiles):
        keeps = []
        for _, _, jb, masked in tiles:
            if masked:
                qpos = i * tile + r0 + lax.broadcasted_iota(jnp.int32, (r1 - r0, tile), 0)
                kpos = jb * tile + lax.broadcasted_iota(jnp.int32, (r1 - r0, tile), 1)
                keeps.append((kpos <= qpos) & (kpos >= FRONT_PAD))
            else:
                keeps.append(None)
        for h in range(hps):
            sl = slice(h * LANES, (h + 1) * LANES)
            q = q_ref[r0:r1, sl]
            ss = []
            for (kr, _, _, _), keep in zip(tiles, keeps):
                s = _dot_nt(q, kr[:, sl])
                ss.append(s if keep is None else jnp.where(keep, s, NEG_INF))
            m_prev = m_scr[h, r0:r1]
            m_cur = jnp.max(ss[0], axis=1, keepdims=True)
            for s in ss[1:]:
                m_cur = jnp.maximum(m_cur, jnp.max(s, axis=1, keepdims=True))
            m_new = jnp.maximum(m_prev, m_cur)
            m_scr[h, r0:r1] = m_new
            pv = None
            for (_, vr, _, _), s in zip(tiles, ss):
                t = _dot(jnp.exp2(s - m_new[:, :1]).astype(BF16), vr[:, sl])
                pv = t if pv is None else pv + t
            acc_scr[h, r0:r1] = jnp.exp2(m_prev - m_new) * acc_scr[h, r0:r1] + pv

    ta = (ka_ref, va_ref, ja)
    tb = (kb_ref, vb_ref, ja + 1)
    if rq == 1:
        two = ja + 1 <= i
        edge = ja + 1 >= i
        pl.when(two & jnp.logical_not(edge))(lambda: body(0, rows, [ta + (False,), tb + (False,)]))
        pl.when(two & edge)(lambda: body(0, rows, [ta + (True,), tb + (True,)]))
        pl.when(jnp.logical_not(two))(lambda: body(0, rows, [ta + (True,)]))
    else:
        pl.when(ja < i)(lambda: body(0, rows, [ta + (False,), tb + (False,)]))
        for d in range(0, rq, 2):
            @pl.when(ja == i + d)
            def _(d=d):
                body(d * tile, (d + 1) * tile, [ta + (True,)])
                body((d + 1) * tile, (d + 2) * tile, [ta + (True,), tb + (True,)])
                if d + 2 < rq:
                    body((d + 2) * tile, rows, [ta + (False,), tb + (False,)])

    @pl.when(ja + 1 >= last)
    def _():
        lane = lax.broadcasted_iota(jnp.int32, (rows, LANES), 1)
        for hp in range(hps // 2):
            a0 = acc_scr[2 * hp]
            a1 = acc_scr[2 * hp + 1]
            lo = a0 / a0[:, MLA_V:MLA_V + 1]
            hi = pltpu.roll(a1 / a1[:, MLA_V:MLA_V + 1], MLA_V, axis=1)
            o_ref[:, hp * LANES:(hp + 1) * LANES] = jnp.where(lane < MLA_V, lo, hi).astype(BF16)


def _attn_call(q, k, v, batch):
    tp = q.shape[0]
    lp = tp // batch
    tile = ROW_TILE
    nb = lp // tile
    qw = MLA_HEADS * LANES
    ow = MLA_HEADS * MLA_V
    q3, k3, v3 = (a.reshape(batch, lp, qw) for a in (q, k, v))

    def call(rq, hps, q_tiles, prev):
        steps = [(i, ja) for i in q_tiles for ja in range(0, i + rq, 2)]
        it = jnp.asarray(np.array([p[0] for p in steps], np.int32))
        jt = jnp.asarray(np.array([p[1] for p in steps], np.int32))
        qmap = lambda b, g, s, it, jt: (b, it[s] // rq, g)
        first = lambda b, g, s, it, jt: (b, jt[s], g)
        second = lambda b, g, s, it, jt: (b, jnp.minimum(jt[s] + 1, it[s] + rq - 1), g)
        gw = hps * LANES
        grid_spec = pltpu.PrefetchScalarGridSpec(
            num_scalar_prefetch=2,
            grid=(batch, MLA_HEADS // hps, len(steps)),
            in_specs=[
                pl.BlockSpec((None, rq * tile, gw), qmap),
                pl.BlockSpec((None, tile, gw), first),
                pl.BlockSpec((None, tile, gw), first),
                pl.BlockSpec((None, tile, gw), second),
                pl.BlockSpec((None, tile, gw), second),
                pl.BlockSpec(memory_space=pl.ANY),
            ],
            out_specs=pl.BlockSpec((None, rq * tile, hps * MLA_V), qmap),
            scratch_shapes=[
                pltpu.VMEM((hps, rq * tile, LANES), F32),
                pltpu.VMEM((hps, rq * tile, LANES), F32),
            ],
        )
        return pl.pallas_call(
            functools.partial(_attn_kernel, tile=tile, rq=rq, hps=hps),
            grid_spec=grid_spec,
            out_shape=jax.ShapeDtypeStruct((batch, lp, ow), BF16),
            input_output_aliases={7: 0},
            compiler_params=pltpu.CompilerParams(dimension_semantics=("parallel", "parallel", "arbitrary"),
                                                 vmem_limit_bytes=VMEM_LIMIT),
            name="mla_attn_x%d" % rq,
        )(it, jt, q3, k3, v3, k3, v3, prev)

    out = jnp.zeros((batch, lp, ow), BF16)
    n_tall = nb // ATT_TALL * ATT_TALL
    if n_tall:
        out = call(ATT_TALL, ATT_TALL_HEADS, list(range(0, n_tall, ATT_TALL)), out)
    if nb > n_tall:
        out = call(1, ATT_TAIL_HEADS, list(range(n_tall, nb)), out)
    return out.reshape(tp, ow)


def _dn_kernel(x_ref, halo_ref, z_ref, misc_ref, cw_ref, par_ref, on_ref, o_ref, buf, s_scr, *, tile, tiles_per_batch):
    t = pl.program_id(0)
    tb = t % tiles_per_batch
    first = tb == 0

    @pl.when(first)
    def _():
        s_scr[...] = jnp.zeros(s_scr.shape, F32)

    buf[0:DN_HALO, :] = jnp.where(first, 0.0, halo_ref[...])
    buf[DN_HALO:DN_HALO + tile, :] = x_ref[...]
    y = cw_ref[0:1, :] * buf[pl.ds(DN_HALO - DN_CONV + 1, tile), :]
    for jj in range(1, DN_CONV):
        y = y + cw_ref[jj:jj + 1, :] * buf[pl.ds(DN_HALO - DN_CONV + 1 + jj, tile), :]
    y = _silu(y)

    misc = misc_ref[...]
    row = tb * tile + lax.broadcasted_iota(jnp.int32, (tile, LANES), 0)
    valid = row >= FRONT_PAD
    beta_all = jnp.where(valid, jax.nn.sigmoid(misc), 0.0)
    g_all = jnp.where(valid, -jnp.exp(par_ref[0:1, :]) * jax.nn.softplus(misc + par_ref[1:2, :]), 0.0)
    rin = lax.broadcasted_iota(jnp.int32, (tile, LANES), 0) % DN_CHUNK
    gc_all = g_all
    sh = 1
    while sh < DN_CHUNK:
        gc_all = gc_all + jnp.where(rin >= sh, pltpu.roll(gc_all, sh, axis=0), 0.0)
        sh *= 2
    gct_all = gc_all.T

    ii = lax.broadcasted_iota(jnp.int32, (DN_CHUNK, DN_CHUNK), 0)
    jj_ = lax.broadcasted_iota(jnp.int32, (DN_CHUNK, DN_CHUNK), 1)
    incl = ii >= jj_
    strict = ii > jj_
    eye = jnp.where(ii == jj_, 1.0, 0.0).astype(F32)
    onorm = on_ref[...]

    nch = tile // DN_CHUNK
    items = []
    for h in range(DN_HEADS):
        hs = slice(h * DN_DK, (h + 1) * DN_DK)
        qh = y[:, hs]
        kh = y[:, DN_HEADS * DN_DK + h * DN_DK:DN_HEADS * DN_DK + (h + 1) * DN_DK]
        vh = y[:, 2 * DN_HEADS * DN_DK + h * DN_DV:2 * DN_HEADS * DN_DK + (h + 1) * DN_DV]
        qh = qh * lax.rsqrt(jnp.sum(qh * qh, axis=-1, keepdims=True) + EPS) * (DN_DK ** -0.5)
        kh = kh * lax.rsqrt(jnp.sum(kh * kh, axis=-1, keepdims=True) + EPS)
        for c in range(nch):
            rs = slice(c * DN_CHUNK, (c + 1) * DN_CHUNK)
            gcol = gc_all[rs, MISC_A + h:MISC_A + h + 1]
            grow = gct_all[MISC_A + h:MISC_A + h + 1, rs]
            bcol = beta_all[rs, MISC_B + h:MISC_B + h + 1]
            gamma = jnp.where(incl, jnp.exp(jnp.where(incl, gcol - grow, 0.0)), 0.0)
            qc = qh[rs]
            kc = kh[rs]
            kb = kc * bcol
            eg = jnp.exp(gcol)
            glast = gcol[DN_CHUNK - 1:DN_CHUNK, :]
            items.append(dict(
                h=h, rs=rs,
                a=jnp.where(strict, _dot_nt(kb, kc) * gamma, 0.0),
                rhs=jnp.concatenate([vh[rs] * bcol, kb * eg], axis=1),
                qk=jnp.where(incl, _dot_nt(qc, kc) * gamma, 0.0),
                q_dec=qc * eg,
                k_dec=kc * jnp.exp(glast - gcol),
                cd=jnp.exp(glast),
            ))
    xs = [-it['a'] for it in items]
    ts = [eye + x for x in xs]
    for _ in range(6):
        xs = [_dot(x, x) for x in xs]
        ts = [t + _dot(t, x) for t, x in zip(ts, xs)]
    for it, tinv in zip(items, ts):
        sol = _dot(tinv, it['rhs'])
        qs = _dot(it['qk'], sol)
        ks = _dot_tn(it['k_dec'], sol)
        it['ob'] = qs[:, :DN_DV]
        it['q_eff'] = it['q_dec'] - qs[:, DN_DV:]
        it['s_add'] = ks[:, :DN_DV]
        it['s_mul'] = ks[:, DN_DV:]
    states = [s_scr[h] for h in range(DN_HEADS)]
    for c in range(nch):
        for h in range(DN_HEADS):
            it = items[h * nch + c]
            st = states[h]
            o = _dot(it['q_eff'], st) + it['ob']
            states[h] = it['cd'] * st + it['s_add'] - _dot(it['s_mul'], st)
            on = _rms(o, onorm)
            zz = z_ref[it['rs'], h * DN_DV:(h + 1) * DN_DV]
            o_ref[it['rs'], h * DN_DV:(h + 1) * DN_DV] = (on * _silu(zz)).astype(BF16)
    for h in range(DN_HEADS):
        s_scr[h] = states[h]


def _dn_call(dn_pre, z, misc, lw, batch):
    tp = dn_pre.shape[0]
    tile = ROW_TILE
    tpb = tp // batch // tile
    hb = tile // DN_HALO
    return pl.pallas_call(
        functools.partial(_dn_kernel, tile=tile, tiles_per_batch=tpb),
        grid=(tp // tile,),
        in_specs=[
            pl.BlockSpec((tile, DN_QKV), lambda t: (t, 0)),
            pl.BlockSpec((DN_HALO, DN_QKV), lambda t: (jnp.maximum(t * hb - 1, 0), 0)),
            pl.BlockSpec((tile, DN_HEADS * DN_DV), lambda t: (t, 0)),
            pl.BlockSpec((tile, LANES), lambda t: (t, 0)),
            _const_spec(lw['dn_cw'].shape),
            _const_spec(lw['dn_par'].shape),
            _const_spec(lw['dn_on'].shape),
        ],
        out_specs=pl.BlockSpec((tile, DN_HEADS * DN_DV), lambda t: (t, 0)),
        out_shape=jax.ShapeDtypeStruct((tp, DN_HEADS * DN_DV), BF16),
        scratch_shapes=[
            pltpu.VMEM((DN_HALO + tile, DN_QKV), F32),
            pltpu.VMEM((DN_HEADS, DN_DK, DN_DV), F32),
        ],
        compiler_params=pltpu.CompilerParams(dimension_semantics=("arbitrary",), vmem_limit_bytes=VMEM_LIMIT),
        name="deltanet",
    )(dn_pre, dn_pre, z, misc, lw['dn_cw'], lw['dn_par'], lw['dn_on'])


def _cv_kernel(x_ref, halo_ref, w_ref, b_ref, g_ref, be_ref, o_ref, buf, *, tile, tiles_per_batch):
    t = pl.program_id(0)
    first = (t % tiles_per_batch) == 0
    buf[0, 0:CV_HALO, :] = jnp.where(first, 0.0, halo_ref[...])
    buf[0, CV_HALO:CV_HALO + tile, :] = x_ref[...]
    span = CV_HALO + tile - SUBLANES
    for r in range(1, SUBLANES):
        buf[r, 0:span, :] = buf[0, pl.ds(r, span), :]
    base = CV_HALO - CV_WIDTH + 1
    acc = None
    for j in range(CV_WIDTH):
        r = (base + j) % SUBLANES
        term = w_ref[j:j + 1, :] * buf[r, pl.ds(base + j - r, tile), :]
        acc = term if acc is None else acc + term
    acc = acc + b_ref[...]
    o_ref[...] = _silu(_layer_norm(acc, g_ref[...], be_ref[...])).astype(BF16)


def _cv_call(glu, lw, batch):
    tp = glu.shape[0]
    tile = ROW_TILE
    tpb = tp // batch // tile
    hb = tile // CV_HALO
    return pl.pallas_call(
        functools.partial(_cv_kernel, tile=tile, tiles_per_batch=tpb),
        grid=(tp // tile,),
        in_specs=[
            pl.BlockSpec((tile, CV_CH), lambda t: (t, 0)),
            pl.BlockSpec((CV_HALO, CV_CH), lambda t: (jnp.maximum(t * hb - 1, 0), 0)),
            _const_spec(lw['cv_w'].shape),
            _const_spec(lw['cv_b'].shape),
            _const_spec(lw['cv_g'].shape),
            _const_spec(lw['cv_be'].shape),
        ],
        out_specs=pl.BlockSpec((tile, CV_CH), lambda t: (t, 0)),
        out_shape=jax.ShapeDtypeStruct((tp, CV_CH), BF16),
        scratch_shapes=[pltpu.VMEM((SUBLANES, CV_HALO + tile, CV_CH), F32)],
        compiler_params=pltpu.CompilerParams(dimension_semantics=("parallel",), vmem_limit_bytes=VMEM_LIMIT),
        name="cv_conv",
    )(glu, glu, lw['cv_w'], lw['cv_b'], lw['cv_g'], lw['cv_be'])


def _merge_kernel(*refs, tile, tiles_per_batch, seq_rows, route, from_frames):
    h_ref, head_ref = (refs[0], refs[1]) if from_frames else (refs[0], None)
    (om_ref, od_ref, oc_ref, wg_ref, wbm_ref, wbd_ref, wbc_ref, wmo_ref, g_ref, b_ref, wrh_ref, wrl_ref,
     o_ref, gate_ref) = refs[2 if from_frames else 1:]
    hf = _load_rows(h_ref, head_ref, tiles_per_batch)
    x = hf.astype(BF16)
    merged = None
    for br, (src, wbr) in enumerate(((om_ref, wbm_ref), (od_ref, wbd_ref), (oc_ref, wbc_ref))):
        gl = _dot(x, wg_ref[:, br * D_MODEL:(br + 1) * D_MODEL])
        yb = _dot(src[...], wbr[...])
        term = jax.nn.sigmoid(gl) * yb
        merged = term if merged is None else merged + term
    mix = _dot(merged.astype(BF16), wmo_ref[...])
    h1 = _layer_norm(ALPHA * hf + mix, g_ref[...], b_ref[...])
    row = (pl.program_id(0) % tiles_per_batch) * tile + lax.broadcasted_iota(jnp.int32, (tile, 1), 0)
    live = (row >= FRONT_PAD) & (row < seq_rows)
    h1 = jnp.where(live, h1, 0.0)
    o_ref[...] = h1
    if route:
        h1_hi = h1.astype(BF16)
        h1_lo = (h1 - h1_hi.astype(F32)).astype(BF16)
        hh = _dot(h1_hi, wrl_ref[...])
        logits = hh[:, :LANES] + hh[:, LANES:] + _dot(h1_lo, wrh_ref[...])
        lane = lax.broadcasted_iota(jnp.int32, (tile, LANES), 1)
        logits = jnp.where(lane < N_EXPERTS, logits, -jnp.inf)
        m1 = jnp.max(logits, axis=1, keepdims=True)
        i1 = jnp.min(jnp.where(logits == m1, lane, LANES), axis=1, keepdims=True)
        rest = jnp.where(lane == i1, -jnp.inf, logits)
        m2 = jnp.max(rest, axis=1, keepdims=True)
        i2 = jnp.min(jnp.where(rest == m2, lane, LANES), axis=1, keepdims=True)
        e2 = jnp.exp(m2 - m1)
        den = 1.0 + e2
        gate_ref[...] = jnp.where(lane == i1, 1.0 / den, jnp.where(lane == i2, e2 / den, 0.0))
    else:
        gate_ref[...] = jnp.ones(gate_ref.shape, F32)


def _merge_call(h, head, o_mla, o_dn, o_cv, lw, batch, seq_rows, route):
    tp = o_mla.shape[0]
    tile = ROW_TILE
    tpb = tp // batch // tile
    row = lambda w: pl.BlockSpec((tile, w), lambda i: (i, 0))
    consts = [lw['wg'], lw['wbm'], lw['wbd'], lw['wbc'], lw['wmo'], lw['ln1_g'], lw['ln1_b'], lw['wr_hi'], lw['wr_lo']]
    if head is None:
        ins, in_specs = [h], [row(D_MODEL)]
    else:
        ins, in_specs = [h, head], [_frames_spec(tile, tpb), _const_spec(head.shape)]
    return pl.pallas_call(
        functools.partial(_merge_kernel, tile=tile, tiles_per_batch=tpb, seq_rows=seq_rows, route=route,
                          from_frames=head is not None),
        grid=(tp // tile,),
        in_specs=in_specs + [row(512), row(512), row(512)] + [_const_spec(a.shape) for a in consts],
        out_specs=[row(D_MODEL), row(LANES)],
        out_shape=[jax.ShapeDtypeStruct((tp, D_MODEL), F32), jax.ShapeDtypeStruct((tp, LANES), F32)],
        compiler_params=pltpu.CompilerParams(dimension_semantics=("parallel",), vmem_limit_bytes=VMEM_LIMIT),
        name="merge_ln1",
    )(*ins, o_mla, o_dn, o_cv, *consts)


def _ffn_kernel(h_ref, gate_ref, wg_ref, wu_ref, wd_ref, g_ref, b_ref, o_ref, acc, *, tile, tiles_per_batch, seq_rows, gated):
    c = pl.program_id(1)
    x = h_ref[...].astype(BF16)
    hid = _silu(_dot(x, wg_ref[0])) * _dot(x, wu_ref[0])
    y = _dot(hid.astype(BF16), wd_ref[0])
    if gated:
        lane = lax.broadcasted_iota(jnp.int32, (tile, LANES), 1)
        y = y * jnp.sum(jnp.where(lane == c, gate_ref[...], 0.0), axis=1, keepdims=True)

    @pl.when(c == 0)
    def _():
        acc[...] = y

    @pl.when(c > 0)
    def _():
        acc[...] = acc[...] + y

    @pl.when(c == pl.num_programs(1) - 1)
    def _():
        h2 = _layer_norm(ALPHA * h_ref[...] + acc[...], g_ref[...], b_ref[...])
        row = (pl.program_id(0) % tiles_per_batch) * tile + lax.broadcasted_iota(jnp.int32, (tile, 1), 0)
        live = (row >= FRONT_PAD) & (row < seq_rows)
        o_ref[...] = jnp.where(live, h2, 0.0)


def _ffn_call(h, gates, wg, wu, wd, ln_g, ln_b, batch, seq_rows, gated, to_frames):
    tp = h.shape[0]
    tile = ROW_TILE
    tpb = tp // batch // tile
    nchunk, _, width = wg.shape
    if to_frames:
        out_spec = _frames_spec(tile, tpb)
        out_rows = tp - batch * tile
        semantics = ("arbitrary", "arbitrary")
    else:
        out_spec = pl.BlockSpec((tile, D_MODEL), lambda i, c: (i, 0))
        out_rows = tp
        semantics = ("parallel", "arbitrary")
    return pl.pallas_call(
        functools.partial(_ffn_kernel, tile=tile, tiles_per_batch=tpb, seq_rows=seq_rows, gated=gated),
        grid=(tp // tile, nchunk),
        in_specs=[
            pl.BlockSpec((tile, D_MODEL), lambda i, c: (i, 0)),
            pl.BlockSpec((tile, LANES), lambda i, c: (i, 0)),
            pl.BlockSpec((1, D_MODEL, width), lambda i, c: (c, 0, 0)),
            pl.BlockSpec((1, D_MODEL, width), lambda i, c: (c, 0, 0)),
            pl.BlockSpec((1, width, D_MODEL), lambda i, c: (c, 0, 0)),
            _const_spec(ln_g.shape),
            _const_spec(ln_b.shape),
        ],
        out_specs=out_spec,
        out_shape=jax.ShapeDtypeStruct((out_rows, D_MODEL), F32),
        scratch_shapes=[pltpu.VMEM((tile, D_MODEL), F32)],
        compiler_params=pltpu.CompilerParams(dimension_semantics=semantics, vmem_limit_bytes=VMEM_LIMIT),
        name="ffn_ln2",
    )(h, gates, wg, wu, wd, ln_g, ln_b)


def _row(v, width=None):
    v = v.astype(F32).reshape(1, -1)
    if width is not None and v.shape[1] < width:
        v = jnp.pad(v, ((0, 0), (0, width - v.shape[1])))
    return v


def _layer_weights(l, w_in, mla_q_norm, mla_w_uq, mla_kv_norm, mla_w_ukv, dn_conv_w, dn_a_log, dn_dt_bias, dn_o_norm,
                   cv_dw_w, cv_dw_b, cv_ln_g, cv_ln_b, w_br_mla, w_br_dn, w_br_cv, w_mix_out, ln1_g, ln1_b, ln2_g, ln2_b,
                   moe_w_router):
    sizes = (MLA_Q_RANK, MLA_KV_RANK, MLA_ROPE, DN_QKV, DN_HEADS * DN_DV, DN_HEADS, DN_HEADS, CV_CH, CV_CH,
             D_MODEL, D_MODEL, D_MODEL)
    offs = np.cumsum((0,) + sizes)
    col = lambda i: w_in[l][:, offs[i]:offs[i + 1]]
    half = MLA_ROPE // 2
    w_kr = col(2)
    w_kr_pair = jnp.concatenate([-w_kr[:, half:], w_kr[:, :half]], axis=1)
    wa = jnp.concatenate([col(0), col(1), w_kr, w_kr_pair, col(5), col(6)], axis=1)
    wa = jnp.pad(wa, ((0, 0), (0, 512 - wa.shape[1])))
    wuq = mla_w_uq[l].reshape(MLA_Q_RANK, MLA_HEADS, MLA_QK)
    zq = jnp.zeros((MLA_Q_RANK, MLA_HEADS, LANES - MLA_QK), F32)
    wq_main = jnp.concatenate([wuq, zq], axis=2)
    wq_pair = jnp.concatenate([jnp.zeros((MLA_Q_RANK, MLA_HEADS, MLA_NOPE), F32), -wuq[:, :, MLA_NOPE + half:],
                               wuq[:, :, MLA_NOPE:MLA_NOPE + half], zq], axis=2)
    wq = jnp.concatenate([wq_main.reshape(MLA_Q_RANK, -1), wq_pair.reshape(MLA_Q_RANK, -1)], axis=1)
    wukv = mla_w_ukv[l].reshape(MLA_KV_RANK, MLA_HEADS, MLA_NOPE + MLA_V)
    wk_lat = jnp.concatenate([wukv[:, :, :MLA_NOPE], jnp.zeros((MLA_KV_RANK, MLA_HEADS, LANES - MLA_NOPE), F32)], axis=2)
    place = np.zeros((LANES, MLA_HEADS, LANES), np.float32)
    for r in range(MLA_ROPE):
        place[r, :, MLA_NOPE + r] = 1.0
    wk = jnp.concatenate([wk_lat.reshape(MLA_KV_RANK, -1), jnp.asarray(place).reshape(LANES, -1)], axis=0)
    wv = jnp.concatenate([wukv[:, :, MLA_NOPE:], jnp.zeros((MLA_KV_RANK, MLA_HEADS, LANES - MLA_V), F32)],
                         axis=2).reshape(MLA_KV_RANK, -1)
    vone = np.zeros((1, MLA_HEADS, LANES), np.float32)
    vone[:, :, MLA_V] = 1.0
    par = jnp.zeros((8, LANES), F32)
    par = par.at[0, MISC_A:MISC_A + DN_HEADS].set(dn_a_log[l].astype(F32))
    par = par.at[1, MISC_A:MISC_A + DN_HEADS].set(dn_dt_bias[l].astype(F32))
    if moe_w_router is None:
        wr = jnp.zeros((D_MODEL, LANES), F32)
    else:
        wr = jnp.pad(moe_w_router.astype(F32), ((0, 0), (0, LANES - N_EXPERTS)))
    wr_hi = wr.astype(BF16)
    wr_lo = jnp.concatenate([wr_hi, (wr - wr_hi.astype(F32)).astype(BF16)], axis=1)
    return dict(
        vone=jnp.asarray(vone.reshape(1, -1)), wr_hi=wr_hi, wr_lo=wr_lo,
        wa=wa.astype(BF16), wq=wq.astype(BF16), wk=wk.astype(BF16), wv=wv.astype(BF16),
        wb=col(3).astype(BF16), wc=col(4).astype(BF16), wd=jnp.concatenate([col(7), col(8)], axis=1).astype(BF16),
        qg=_row(mla_q_norm[l]), kvg=_row(mla_kv_norm[l]),
        dn_cw=jnp.pad(dn_conv_w[l].astype(F32), ((0, 8 - DN_CONV), (0, 0))), dn_par=par, dn_on=_row(dn_o_norm[l]),
        cv_w=jnp.pad(cv_dw_w[l].astype(F32), ((0, 32 - CV_WIDTH), (0, 0))), cv_b=_row(cv_dw_b[l]),
        cv_g=_row(cv_ln_g[l]), cv_be=_row(cv_ln_b[l]),
        wg=jnp.concatenate([col(9), col(10), col(11)], axis=1).astype(BF16),
        wbm=w_br_mla[l].astype(BF16), wbd=w_br_dn[l].astype(BF16), wbc=w_br_cv[l].astype(BF16),
        wmo=w_mix_out[l].astype(BF16), ln1_g=_row(ln1_g[l]), ln1_b=_row(ln1_b[l]),
        ln2_g=_row(ln2_g[l]), ln2_b=_row(ln2_b[l]),
    )


def _rope_tables(lp):
    pos = jnp.maximum(jnp.arange(lp, dtype=jnp.int32) - FRONT_PAD, 0).astype(F32)
    inv_freq = ROPE_THETA ** (-jnp.arange(0, MLA_ROPE, 2, dtype=F32) / MLA_ROPE)
    ang = pos[:, None] * inv_freq[None, :]
    cos2 = jnp.concatenate([jnp.cos(ang), jnp.cos(ang)], axis=1)
    sin2 = jnp.concatenate([jnp.sin(ang), jnp.sin(ang)], axis=1)
    scale = MLA_QK ** -0.5 * np.log2(np.e)
    zpad = jnp.zeros((lp, LANES - MLA_QK), F32)
    ct = jnp.concatenate([jnp.full((lp, MLA_NOPE), scale, F32), scale * cos2, zpad], axis=1)
    st = jnp.concatenate([jnp.zeros((lp, MLA_NOPE), F32), scale * sin2, zpad], axis=1)
    kc = jnp.concatenate([cos2, sin2, jnp.zeros((lp, LANES - 2 * MLA_ROPE), F32)], axis=1)
    return ct, st, kc


def kernel(x, meta_tokens, w_in, mla_q_norm, mla_w_uq, mla_kv_norm, mla_w_ukv, dn_conv_w, dn_a_log, dn_dt_bias, dn_o_norm, cv_dw_w, cv_dw_b, cv_ln_g, cv_ln_b, w_br_mla, w_br_dn, w_br_cv, w_mix_out, ln1_g, ln1_b, ln2_g, ln2_b, ffn_w_gate, ffn_w_up, ffn_w_down, moe_w_router, moe_w_gate, moe_w_up, moe_w_down):
    batch, seq, _ = x.shape
    assert FRONT_PAD + N_META == ROW_TILE and seq % ROW_TILE == 0 and seq >= ROW_TILE
    lp = seq_rows = ROW_TILE + seq
    tp = batch * lp
    head = jnp.concatenate([jnp.zeros((FRONT_PAD, D_MODEL), F32), meta_tokens.astype(F32)], axis=0)
    h = x.astype(F32).reshape(batch * seq, D_MODEL)
    ct, st, kc = _rope_tables(lp)
    tabs = dict(ct=jnp.tile(ct, (batch, 1)), st=jnp.tile(st, (batch, 1)), kc=jnp.tile(kc, (batch, 1)))
    for l in range(DEPTH):
        moe = l % 2 == 1
        lw = _layer_weights(l, w_in, mla_q_norm, mla_w_uq, mla_kv_norm, mla_w_ukv, dn_conv_w, dn_a_log, dn_dt_bias,
                            dn_o_norm, cv_dw_w, cv_dw_b, cv_ln_g, cv_ln_b, w_br_mla, w_br_dn, w_br_cv, w_mix_out,
                            ln1_g, ln1_b, ln2_g, ln2_b, moe_w_router[l // 2] if moe else None)
        src_head = head if l == 0 else None
        q, k, v, dn_pre, z, glu, misc = _proj_call(h, src_head, lw, tabs, tp, batch)
        o_mla = _attn_call(q, k, v, batch)
        o_dn = _dn_call(dn_pre, z, misc, lw, batch)
        o_cv = _cv_call(glu, lw, batch)
        h1, gates = _merge_call(h, src_head, o_mla, o_dn, o_cv, lw, batch, seq_rows, moe)
        if moe:
            wg, wu, wd = moe_w_gate[l // 2], moe_w_up[l // 2], moe_w_down[l // 2]
        else:
            nck = D_FF // D_FF_EXPERT
            wg = ffn_w_gate[l // 2].reshape(D_MODEL, nck, D_FF_EXPERT).transpose(1, 0, 2)
            wu = ffn_w_up[l // 2].reshape(D_MODEL, nck, D_FF_EXPERT).transpose(1, 0, 2)
            wd = ffn_w_down[l // 2].reshape(nck, D_FF_EXPERT, D_MODEL)
        h = _ffn_call(h1, gates, wg.astype(BF16), wu.astype(BF16), wd.astype(BF16), lw['ln2_g'], lw['ln2_b'],
                      batch, seq_rows, moe, l == DEPTH - 1)
    return h.reshape(batch, seq, D_MODEL).astype(x.dtype)
```

```python
import functools

import jax
import jax.numpy as jnp
import numpy as np
from jax import lax
from jax.experimental import pallas as pl
from jax.experimental.pallas import tpu as pltpu

D_MODEL = 1024
DEPTH = 2
N_META = 16
MLA_HEADS = 8
MLA_Q_RANK = 256
MLA_KV_RANK = 128
MLA_NOPE = 64
MLA_ROPE = 32
MLA_V = 64
ROPE_THETA = 10000.0
DN_HEADS = 4
DN_DK = 128
DN_DV = 128
DN_CONV = 4
CV_CH = 512
CV_WIDTH = 31
D_FF = 2816
N_EXPERTS = 8
D_FF_EXPERT = 1408
ALPHA = (2 * DEPTH) ** 0.25
EPS = 1e-6
NEG_INF = -1e30
MLA_QK = MLA_NOPE + MLA_ROPE
DN_QKV = 2 * DN_HEADS * DN_DK + DN_HEADS * DN_DV

LANES = 128
SUBLANES = 8
FRONT_PAD = 496
ROW_TILE = 512
ATT_TALL = 4
ATT_TALL_HEADS = 2
ATT_TAIL_HEADS = 4
DN_CHUNK = 128
CV_HALO = 32
DN_HALO = 8
MISC_B = 64
MISC_A = 68
VMEM_LIMIT = 56 * 1024 * 1024

BF16 = jnp.bfloat16
F32 = jnp.float32


def _dot(a, b):
    return jnp.dot(a, b, preferred_element_type=F32)


def _dot_nt(a, b):
    return lax.dot_general(a, b, (((1,), (1,)), ((), ())), preferred_element_type=F32)


def _dot_tn(a, b):
    return lax.dot_general(a, b, (((0,), (0,)), ((), ())), preferred_element_type=F32)


def _rms(x, g):
    return x * lax.rsqrt(jnp.mean(x * x, axis=-1, keepdims=True) + EPS) * g


def _layer_norm(x, g, b):
    mu = jnp.mean(x, axis=-1, keepdims=True)
    xc = x - mu
    var = jnp.mean(xc * xc, axis=-1, keepdims=True)
    return xc * lax.rsqrt(var + EPS) * g + b


def _silu(x):
    return x * jax.nn.sigmoid(x)


def _const_spec(shape):
    return pl.BlockSpec(shape, lambda *_: (0,) * len(shape))


def _frames_spec(tile, tpb):
    return pl.BlockSpec((tile, D_MODEL), lambda i, *_: ((i // tpb) * (tpb - 1) + jnp.maximum(i % tpb - 1, 0), 0))


def _load_rows(h_ref, head_ref, tiles_per_batch):
    if head_ref is None:
        return h_ref[...]
    return jnp.where(pl.program_id(0) % tiles_per_batch == 0, head_ref[...], h_ref[...])


def _proj_kernel(*refs, tiles_per_batch, from_frames):
    h_ref, head_ref = (refs[0], refs[1]) if from_frames else (refs[0], None)
    (wa_ref, wq_ref, wk_ref, wv_ref, wb_ref, wc_ref, wd_ref, qg_ref, kvg_ref, vone_ref, ct_ref, st_ref, kc_ref,
     q_ref, k_ref, v_ref, dn_ref, z_ref, glu_ref, misc_ref) = refs[2 if from_frames else 1:]
    x = _load_rows(h_ref, head_ref, tiles_per_batch).astype(BF16)
    pa = _dot(x, wa_ref[...])
    cq = pa[:, :MLA_Q_RANK]
    ckv = pa[:, MLA_Q_RANK:MLA_Q_RANK + MLA_KV_RANK]
    blk = pa[:, MLA_Q_RANK + MLA_KV_RANK:]
    cqn = _rms(cq, qg_ref[...])
    qq = _dot(cqn.astype(BF16), wq_ref[...])
    ct = ct_ref[...]
    st = st_ref[...]
    hw = MLA_HEADS * LANES
    qone = jnp.where(lax.broadcasted_iota(jnp.int32, (1, LANES), 1) == MLA_QK, 1.0, 0.0)
    for h in range(MLA_HEADS):
        sl = slice(h * LANES, (h + 1) * LANES)
        sp = slice(hw + h * LANES, hw + (h + 1) * LANES)
        q_ref[:, sl] = (qq[:, sl] * ct + qq[:, sp] * st + qone).astype(BF16)
    ckvn = _rms(ckv, kvg_ref[...])
    prod = blk * kc_ref[...]
    krr = prod + pltpu.roll(prod, LANES - MLA_ROPE, axis=1)
    kin = jnp.concatenate([ckvn, krr], axis=1).astype(BF16)
    kk = _dot(kin, wk_ref[...])
    krow = lax.broadcasted_iota(jnp.int32, kk.shape, 0)
    klane = lax.broadcasted_iota(jnp.int32, kk.shape, 1) % LANES
    padded = (pl.program_id(0) % tiles_per_batch == 0) & (krow < FRONT_PAD) & (klane == MLA_QK)
    k_ref[...] = jnp.where(padded, NEG_INF, kk).astype(BF16)
    v_ref[...] = (_dot(ckvn.astype(BF16), wv_ref[...]) + vone_ref[...]).astype(BF16)
    dn_ref[...] = _dot(x, wb_ref[...])
    z_ref[...] = _dot(x, wc_ref[...])
    cv = _dot(x, wd_ref[...])
    glu_ref[...] = cv[:, :CV_CH] * jax.nn.sigmoid(cv[:, CV_CH:])
    misc_ref[...] = blk


def _proj_call(h, head, lw, tabs, tp, batch):
    tm = ROW_TILE
    tpb = tp // batch // tm
    row = lambda w: pl.BlockSpec((tm, w), lambda i: (i, 0))
    consts = [lw['wa'], lw['wq'], lw['wk'], lw['wv'], lw['wb'], lw['wc'], lw['wd'], lw['qg'], lw['kvg'], lw['vone']]
    if head is None:
        ins, in_specs = [h], [row(D_MODEL)]
    else:
        ins, in_specs = [h, head], [_frames_spec(tm, tpb), _const_spec(head.shape)]
    ins += consts + [tabs['ct'], tabs['st'], tabs['kc']]
    in_specs += [_const_spec(a.shape) for a in consts] + [row(LANES)] * 3
    out_shape = [
        jax.ShapeDtypeStruct((tp, MLA_HEADS * LANES), BF16),
        jax.ShapeDtypeStruct((tp, MLA_HEADS * LANES), BF16),
        jax.ShapeDtypeStruct((tp, MLA_HEADS * LANES), BF16),
        jax.ShapeDtypeStruct((tp, DN_QKV), F32),
        jax.ShapeDtypeStruct((tp, DN_HEADS * DN_DV), F32),
        jax.ShapeDtypeStruct((tp, CV_CH), F32),
        jax.ShapeDtypeStruct((tp, LANES), F32),
    ]
    out_specs = [row(s.shape[1]) for s in out_shape]
    return pl.pallas_call(
        functools.partial(_proj_kernel, tiles_per_batch=tpb, from_frames=head is not None),
        grid=(tp // tm,),
        in_specs=in_specs,
        out_specs=out_specs,
        out_shape=out_shape,
        compiler_params=pltpu.CompilerParams(dimension_semantics=("parallel",), vmem_limit_bytes=VMEM_LIMIT),
        name="proj",
    )(*ins)


def _attn_kernel(it_ref, jt_ref, q_ref, ka_ref, va_ref, kb_ref, vb_ref, *rest, tile, rq, hps):
    o_ref, m_scr, acc_scr = rest[-3:]
    step = pl.program_id(2)
    i = it_ref[step]
    last = i + rq - 1
    ja = jt_ref[step]
    rows = rq * tile

    @pl.when(ja == 0)
    def _():
        m_scr[...] = jnp.full(m_scr.shape, NEG_INF, F32)
        acc_scr[...] = jnp.zeros(acc_scr.shape, F32)

    def body(r0, r1, tiles):
        keeps = []
        for _, _, jb, masked in tiles:
            if masked:
                qpos = i * tile + r0 + lax.broadcasted_iota(jnp.int32, (r1 - r0, tile), 0)
                kpos = jb * tile + lax.broadcasted_iota(jnp.int32, (r1 - r0, tile), 1)
                keeps.append((kpos <= qpos) & (kpos >= FRONT_PAD))
            else:
                keeps.append(None)
        for h in range(hps):
            sl = slice(h * LANES, (h + 1) * LANES)
            q = q_ref[r0:r1, sl]
            ss = []
            for (kr, _, _, _), keep in zip(tiles, keeps):
                s = _dot_nt(q, kr[:, sl])
                ss.append(s if keep is None else jnp.where(keep, s, NEG_INF))
            m_prev = m_scr[h, r0:r1]
            m_cur = jnp.max(ss[0], axis=1, keepdims=True)
            for s in ss[1:]:
                m_cur = jnp.maximum(m_cur, jnp.max(s, axis=1, keepdims=True))
            m_new = jnp.maximum(m_prev, m_cur)
            m_scr[h, r0:r1] = m_new
            pv = None
            for (_, vr, _, _), s in zip(tiles, ss):
                t = _dot(jnp.exp2(s - m_new[:, :1]).astype(BF16), vr[:, sl])
                pv = t if pv is None else pv + t
            acc_scr[h, r0:r1] = jnp.exp2(m_prev - m_new) * acc_scr[h, r0:r1] + pv

    ta = (ka_ref, va_ref, ja)
    tb = (kb_ref, vb_ref, ja + 1)
    if rq == 1:
        two = ja + 1 <= i
        edge = ja + 1 >= i
        pl.when(two & jnp.logical_not(edge))(lambda: body(0, rows, [ta + (False,), tb + (False,)]))
        pl.when(two & edge)(lambda: body(0, rows, [ta + (True,), tb + (True,)]))
        pl.when(jnp.logical_not(two))(lambda: body(0, rows, [ta + (True,)]))
    else:
        pl.when(ja < i)(lambda: body(0, rows, [ta + (False,), tb + (False,)]))
        for d in range(0, rq, 2):
            @pl.when(ja == i + d)
            def _(d=d):
                body(d * tile, (d + 1) * tile, [ta + (True,)])
                body((d + 1) * tile, (d + 2) * tile, [ta + (True,), tb + (True,)])
                if d + 2 < rq:
                    body((d + 2) * tile, rows, [ta + (False,), tb + (False,)])

    @pl.when(ja + 1 >= last)
    def _():
        lane = lax.broadcasted_iota(jnp.int32, (rows, LANES), 1)
        for hp in range(hps // 2):
            a0 = acc_scr[2 * hp]
            a1 = acc_scr[2 * hp + 1]
            lo = a0 / a0[:, MLA_V:MLA_V + 1]
            hi = pltpu.roll(a1 / a1[:, MLA_V:MLA_V + 1], MLA_V, axis=1)
            o_ref[:, hp * LANES:(hp + 1) * LANES] = jnp.where(lane < MLA_V, lo, hi).astype(BF16)


def _attn_call(q, k, v, batch):
    tp = q.shape[0]
    lp = tp // batch
    tile = ROW_TILE
    nb = lp // tile
    qw = MLA_HEADS * LANES
    ow = MLA_HEADS * MLA_V
    q3, k3, v3 = (a.reshape(batch, lp, qw) for a in (q, k, v))

    def call(rq, hps, q_tiles, prev):
        steps = [(i, ja) for i in q_tiles for ja in range(0, i + rq, 2)]
        it = jnp.asarray(np.array([p[0] for p in steps], np.int32))
        jt = jnp.asarray(np.array([p[1] for p in steps], np.int32))
        qmap = lambda b, g, s, it, jt: (b, it[s] // rq, g)
        first = lambda b, g, s, it, jt: (b, jt[s], g)
        second = lambda b, g, s, it, jt: (b, jnp.minimum(jt[s] + 1, it[s] + rq - 1), g)
        gw = hps * LANES
        grid_spec = pltpu.PrefetchScalarGridSpec(
            num_scalar_prefetch=2,
            grid=(batch, MLA_HEADS // hps, len(steps)),
            in_specs=[
                pl.BlockSpec((None, rq * tile, gw), qmap),
                pl.BlockSpec((None, tile, gw), first),
                pl.BlockSpec((None, tile, gw), first),
                pl.BlockSpec((None, tile, gw), second),
                pl.BlockSpec((None, tile, gw), second),
                pl.BlockSpec(memory_space=pl.ANY),
            ],
            out_specs=pl.BlockSpec((None, rq * tile, hps * MLA_V), qmap),
            scratch_shapes=[
                pltpu.VMEM((hps, rq * tile, LANES), F32),
                pltpu.VMEM((hps, rq * tile, LANES), F32),
            ],
        )
        return pl.pallas_call(
            functools.partial(_attn_kernel, tile=tile, rq=rq, hps=hps),
            grid_spec=grid_spec,
            out_shape=jax.ShapeDtypeStruct((batch, lp, ow), BF16),
            input_output_aliases={7: 0},
            compiler_params=pltpu.CompilerParams(dimension_semantics=("parallel", "parallel", "arbitrary"),
                                                 vmem_limit_bytes=VMEM_LIMIT),
            name="mla_attn_x%d" % rq,
        )(it, jt, q3, k3, v3, k3, v3, prev)

    out = jnp.zeros((batch, lp, ow), BF16)
    n_tall = nb // ATT_TALL * ATT_TALL
    if n_tall:
        out = call(ATT_TALL, ATT_TALL_HEADS, list(range(0, n_tall, ATT_TALL)), out)
    if nb > n_tall:
        out = call(1, ATT_TAIL_HEADS, list(range(n_tall, nb)), out)
    return out.reshape(tp, ow)


def _dn_kernel(x_ref, halo_ref, z_ref, misc_ref, cw_ref, par_ref, on_ref, o_ref, buf, s_scr, *, tile, tiles_per_batch):
    t = pl.program_id(0)
    tb = t % tiles_per_batch
    first = tb == 0

    @pl.when(first)
    def _():
        s_scr[...] = jnp.zeros(s_scr.shape, F32)

    buf[0:DN_HALO, :] = jnp.where(first, 0.0, halo_ref[...])
    buf[DN_HALO:DN_HALO + tile, :] = x_ref[...]
    y = cw_ref[0:1, :] * buf[pl.ds(DN_HALO - DN_CONV + 1, tile), :]
    for jj in range(1, DN_CONV):
        y = y + cw_ref[jj:jj + 1, :] * buf[pl.ds(DN_HALO - DN_CONV + 1 + jj, tile), :]
    y = _silu(y)

    misc = misc_ref[...]
    row = tb * tile + lax.broadcasted_iota(jnp.int32, (tile, LANES), 0)
    valid = row >= FRONT_PAD
    beta_all = jnp.where(valid, jax.nn.sigmoid(misc), 0.0)
    g_all = jnp.where(valid, -jnp.exp(par_ref[0:1, :]) * jax.nn.softplus(misc + par_ref[1:2, :]), 0.0)
    rin = lax.broadcasted_iota(jnp.int32, (tile, LANES), 0) % DN_CHUNK
    gc_all = g_all
    sh = 1
    while sh < DN_CHUNK:
        gc_all = gc_all + jnp.where(rin >= sh, pltpu.roll(gc_all, sh, axis=0), 0.0)
        sh *= 2
    gct_all = gc_all.T

    ii = lax.broadcasted_iota(jnp.int32, (DN_CHUNK, DN_CHUNK), 0)
    jj_ = lax.broadcasted_iota(jnp.int32, (DN_CHUNK, DN_CHUNK), 1)
    incl = ii >= jj_
    strict = ii > jj_
    eye = jnp.where(ii == jj_, 1.0, 0.0).astype(F32)
    onorm = on_ref[...]

    nch = tile // DN_CHUNK
    items = []
    for h in range(DN_HEADS):
        hs = slice(h * DN_DK, (h + 1) * DN_DK)
        qh = y[:, hs]
        kh = y[:, DN_HEADS * DN_DK + h * DN_DK:DN_HEADS * DN_DK + (h + 1) * DN_DK]
        vh = y[:, 2 * DN_HEADS * DN_DK + h * DN_DV:2 * DN_HEADS * DN_DK + (h + 1) * DN_DV]
        qh = qh * lax.rsqrt(jnp.sum(qh * qh, axis=-1, keepdims=True) + EPS) * (DN_DK ** -0.5)
        kh = kh * lax.rsqrt(jnp.sum(kh * kh, axis=-1, keepdims=True) + EPS)
        for c in range(nch):
            rs = slice(c * DN_CHUNK, (c + 1) * DN_CHUNK)
            gcol = gc_all[rs, MISC_A + h:MISC_A + h + 1]
            grow = gct_all[MISC_A + h:MISC_A + h + 1, rs]
            bcol = beta_all[rs, MISC_B + h:MISC_B + h + 1]
            gamma = jnp.where(incl, jnp.exp(jnp.where(incl, gcol - grow, 0.0)), 0.0)
            qc = qh[rs]
            kc = kh[rs]
            kb = kc * bcol
            eg = jnp.exp(gcol)
            glast = gcol[DN_CHUNK - 1:DN_CHUNK, :]
            items.append(dict(
                h=h, rs=rs,
                a=jnp.where(strict, _dot_nt(kb, kc) * gamma, 0.0),
                rhs=jnp.concatenate([vh[rs] * bcol, kb * eg], axis=1),
                qk=jnp.where(incl, _dot_nt(qc, kc) * gamma, 0.0),
                q_dec=qc * eg,
                k_dec=kc * jnp.exp(glast - gcol),
                cd=jnp.exp(glast),
            ))
    xs = [-it['a'] for it in items]
    ts = [eye + x for x in xs]
    for _ in range(6):
        xs = [_dot(x, x) for x in xs]
        ts = [t + _dot(t, x) for t, x in zip(ts, xs)]
    for it, tinv in zip(items, ts):
        sol = _dot(tinv, it['rhs'])
        qs = _dot(it['qk'], sol)
        ks = _dot_tn(it['k_dec'], sol)
        it['ob'] = qs[:, :DN_DV]
        it['q_eff'] = it['q_dec'] - qs[:, DN_DV:]
        it['s_add'] = ks[:, :DN_DV]
        it['s_mul'] = ks[:, DN_DV:]
    states = [s_scr[h] for h in range(DN_HEADS)]
    for c in range(nch):
        for h in range(DN_HEADS):
            it = items[h * nch + c]
            st = states[h]
            o = _dot(it['q_eff'], st) + it['ob']
            states[h] = it['cd'] * st + it['s_add'] - _dot(it['s_mul'], st)
            on = _rms(o, onorm)
            zz = z_ref[it['rs'], h * DN_DV:(h + 1) * DN_DV]
            o_ref[it['rs'], h * DN_DV:(h + 1) * DN_DV] = (on * _silu(zz)).astype(BF16)
    for h in range(DN_HEADS):
        s_scr[h] = states[h]


def _dn_call(dn_pre, z, misc, lw, batch):
    tp = dn_pre.shape[0]
    tile = ROW_TILE
    tpb = tp // batch // tile
    hb = tile // DN_HALO
    return pl.pallas_call(
        functools.partial(_dn_kernel, tile=tile, tiles_per_batch=tpb),
        grid=(tp // tile,),
        in_specs=[
            pl.BlockSpec((tile, DN_QKV), lambda t: (t, 0)),
            pl.BlockSpec((DN_HALO, DN_QKV), lambda t: (jnp.maximum(t * hb - 1, 0), 0)),
            pl.BlockSpec((tile, DN_HEADS * DN_DV), lambda t: (t, 0)),
            pl.BlockSpec((tile, LANES), lambda t: (t, 0)),
            _const_spec(lw['dn_cw'].shape),
            _const_spec(lw['dn_par'].shape),
            _const_spec(lw['dn_on'].shape),
        ],
        out_specs=pl.BlockSpec((tile, DN_HEADS * DN_DV), lambda t: (t, 0)),
        out_shape=jax.ShapeDtypeStruct((tp, DN_HEADS * DN_DV), BF16),
        scratch_shapes=[
            pltpu.VMEM((DN_HALO + tile, DN_QKV), F32),
            pltpu.VMEM((DN_HEADS, DN_DK, DN_DV), F32),
        ],
        compiler_params=pltpu.CompilerParams(dimension_semantics=("arbitrary",), vmem_limit_bytes=VMEM_LIMIT),
        name="deltanet",
    )(dn_pre, dn_pre, z, misc, lw['dn_cw'], lw['dn_par'], lw['dn_on'])


def _cv_branch(x_ref, halo_ref, w_ref, b_ref, g_ref, be_ref, buf, *, tile, tiles_per_batch):
    first = (pl.program_id(0) % tiles_per_batch) == 0
    buf[0, 0:CV_HALO, :] = jnp.where(first, 0.0, halo_ref[...])
    buf[0, CV_HALO:CV_HALO + tile, :] = x_ref[...]
    span = CV_HALO + tile - SUBLANES
    for r in range(1, SUBLANES):
        buf[r, 0:span, :] = buf[0, pl.ds(r, span), :]
    base = CV_HALO - CV_WIDTH + 1
    acc = None
    for j in range(CV_WIDTH):
        r = (base + j) % SUBLANES
        term = w_ref[j:j + 1, :] * buf[r, pl.ds(base + j - r, tile), :]
        acc = term if acc is None else acc + term
    acc = acc + b_ref[...]
    return _silu(_layer_norm(acc, g_ref[...], be_ref[...])).astype(BF16)


def _merge_kernel(*refs, tile, tiles_per_batch, seq_rows, route, from_frames):
    h_ref, head_ref = (refs[0], refs[1]) if from_frames else (refs[0], None)
    (om_ref, od_ref, glu_ref, halo_ref, cvw_ref, cvb_ref, cvg_ref, cvbe_ref, wg_ref, wbm_ref, wbd_ref, wbc_ref,
     wmo_ref, g_ref, b_ref, wrh_ref, wrl_ref, o_ref, gate_ref, cv_buf) = refs[2 if from_frames else 1:]
    hf = _load_rows(h_ref, head_ref, tiles_per_batch)
    x = hf.astype(BF16)
    o_cv = _cv_branch(glu_ref, halo_ref, cvw_ref, cvb_ref, cvg_ref, cvbe_ref, cv_buf, tile=tile,
                      tiles_per_batch=tiles_per_batch)
    merged = None
    for br, (src, wbr) in enumerate(((om_ref[...], wbm_ref), (od_ref[...], wbd_ref), (o_cv, wbc_ref))):
        gl = _dot(x, wg_ref[:, br * D_MODEL:(br + 1) * D_MODEL])
        yb = _dot(src, wbr[...])
        term = jax.nn.sigmoid(gl) * yb
        merged = term if merged is None else merged + term
    mix = _dot(merged.astype(BF16), wmo_ref[...])
    h1 = _layer_norm(ALPHA * hf + mix, g_ref[...], b_ref[...])
    row = (pl.program_id(0) % tiles_per_batch) * tile + lax.broadcasted_iota(jnp.int32, (tile, 1), 0)
    live = (row >= FRONT_PAD) & (row < seq_rows)
    h1 = jnp.where(live, h1, 0.0)
    o_ref[...] = h1
    if route:
        h1_hi = h1.astype(BF16)
        h1_lo = (h1 - h1_hi.astype(F32)).astype(BF16)
        hh = _dot(h1_hi, wrl_ref[...])
        logits = hh[:, :LANES] + hh[:, LANES:] + _dot(h1_lo, wrh_ref[...])
        lane = lax.broadcasted_iota(jnp.int32, (tile, LANES), 1)
        logits = jnp.where(lane < N_EXPERTS, logits, -jnp.inf)
        m1 = jnp.max(logits, axis=1, keepdims=True)
        i1 = jnp.min(jnp.where(logits == m1, lane, LANES), axis=1, keepdims=True)
        rest = jnp.where(lane == i1, -jnp.inf, logits)
        m2 = jnp.max(rest, axis=1, keepdims=True)
        i2 = jnp.min(jnp.where(rest == m2, lane, LANES), axis=1, keepdims=True)
        e2 = jnp.exp(m2 - m1)
        den = 1.0 + e2
        gate_ref[...] = jnp.where(lane == i1, 1.0 / den, jnp.where(lane == i2, e2 / den, 0.0))
    else:
        gate_ref[...] = jnp.ones(gate_ref.shape, F32)


def _merge_call(h, head, o_mla, o_dn, glu, lw, batch, seq_rows, route):
    tp = o_mla.shape[0]
    tile = ROW_TILE
    tpb = tp // batch // tile
    hb = tile // CV_HALO
    row = lambda w: pl.BlockSpec((tile, w), lambda i: (i, 0))
    halo = pl.BlockSpec((CV_HALO, CV_CH), lambda i: (jnp.maximum(i * hb - 1, 0), 0))
    consts = [lw['cv_w'], lw['cv_b'], lw['cv_g'], lw['cv_be'], lw['wg'], lw['wbm'], lw['wbd'], lw['wbc'], lw['wmo'],
              lw['ln1_g'], lw['ln1_b'], lw['wr_hi'], lw['wr_lo']]
    if head is None:
        ins, in_specs = [h], [row(D_MODEL)]
    else:
        ins, in_specs = [h, head], [_frames_spec(tile, tpb), _const_spec(head.shape)]
    return pl.pallas_call(
        functools.partial(_merge_kernel, tile=tile, tiles_per_batch=tpb, seq_rows=seq_rows, route=route,
                          from_frames=head is not None),
        grid=(tp // tile,),
        in_specs=in_specs + [row(512), row(512), row(CV_CH), halo] + [_const_spec(a.shape) for a in consts],
        out_specs=[row(D_MODEL), row(LANES)],
        out_shape=[jax.ShapeDtypeStruct((tp, D_MODEL), F32), jax.ShapeDtypeStruct((tp, LANES), F32)],
        scratch_shapes=[pltpu.VMEM((SUBLANES, CV_HALO + tile, CV_CH), F32)],
        compiler_params=pltpu.CompilerParams(dimension_semantics=("parallel",), vmem_limit_bytes=VMEM_LIMIT),
        name="merge_ln1",
    )(*ins, o_mla, o_dn, glu, glu, *consts)


def _ffn_kernel(h_ref, gate_ref, wg_ref, wu_ref, wd_ref, g_ref, b_ref, o_ref, acc, *, tile, tiles_per_batch, seq_rows, gated):
    c = pl.program_id(1)
    x = h_ref[...].astype(BF16)
    hid = _silu(_dot(x, wg_ref[0])) * _dot(x, wu_ref[0])
    y = _dot(hid.astype(BF16), wd_ref[0])
    if gated:
        lane = lax.broadcasted_iota(jnp.int32, (tile, LANES), 1)
        y = y * jnp.sum(jnp.where(lane == c, gate_ref[...], 0.0), axis=1, keepdims=True)

    @pl.when(c == 0)
    def _():
        acc[...] = y

    @pl.when(c > 0)
    def _():
        acc[...] = acc[...] + y

    @pl.when(c == pl.num_programs(1) - 1)
    def _():
        h2 = _layer_norm(ALPHA * h_ref[...] + acc[...], g_ref[...], b_ref[...])
        row = (pl.program_id(0) % tiles_per_batch) * tile + lax.broadcasted_iota(jnp.int32, (tile, 1), 0)
        live = (row >= FRONT_PAD) & (row < seq_rows)
        o_ref[...] = jnp.where(live, h2, 0.0)


def _ffn_call(h, gates, wg, wu, wd, ln_g, ln_b, batch, seq_rows, gated, to_frames):
    tp = h.shape[0]
    tile = ROW_TILE
    tpb = tp // batch // tile
    nchunk, _, width = wg.shape
    if to_frames:
        out_spec = _frames_spec(tile, tpb)
        out_rows = tp - batch * tile
        semantics = ("arbitrary", "arbitrary")
    else:
        out_spec = pl.BlockSpec((tile, D_MODEL), lambda i, c: (i, 0))
        out_rows = tp
        semantics = ("parallel", "arbitrary")
    return pl.pallas_call(
        functools.partial(_ffn_kernel, tile=tile, tiles_per_batch=tpb, seq_rows=seq_rows, gated=gated),
        grid=(tp // tile, nchunk),
        in_specs=[
            pl.BlockSpec((tile, D_MODEL), lambda i, c: (i, 0)),
            pl.BlockSpec((tile, LANES), lambda i, c: (i, 0)),
            pl.BlockSpec((1, D_MODEL, width), lambda i, c: (c, 0, 0)),
            pl.BlockSpec((1, D_MODEL, width), lambda i, c: (c, 0, 0)),
            pl.BlockSpec((1, width, D_MODEL), lambda i, c: (c, 0, 0)),
            _const_spec(ln_g.shape),
            _const_spec(ln_b.shape),
        ],
        out_specs=out_spec,
        out_shape=jax.ShapeDtypeStruct((out_rows, D_MODEL), F32),
        scratch_shapes=[pltpu.VMEM((tile, D_MODEL), F32)],
        compiler_params=pltpu.CompilerParams(dimension_semantics=semantics, vmem_limit_bytes=VMEM_LIMIT),
        name="ffn_ln2",
    )(h, gates, wg, wu, wd, ln_g, ln_b)


def _row(v, width=None):
    v = v.astype(F32).reshape(1, -1)
    if width is not None and v.shape[1] < width:
        v = jnp.pad(v, ((0, 0), (0, width - v.shape[1])))
    return v


def _layer_weights(l, w_in, mla_q_norm, mla_w_uq, mla_kv_norm, mla_w_ukv, dn_conv_w, dn_a_log, dn_dt_bias, dn_o_norm,
                   cv_dw_w, cv_dw_b, cv_ln_g, cv_ln_b, w_br_mla, w_br_dn, w_br_cv, w_mix_out, ln1_g, ln1_b, ln2_g, ln2_b,
                   moe_w_router):
    sizes = (MLA_Q_RANK, MLA_KV_RANK, MLA_ROPE, DN_QKV, DN_HEADS * DN_DV, DN_HEADS, DN_HEADS, CV_CH, CV_CH,
             D_MODEL, D_MODEL, D_MODEL)
    offs = np.cumsum((0,) + sizes)
    col = lambda i: w_in[l][:, offs[i]:offs[i + 1]]
    half = MLA_ROPE // 2
    w_kr = col(2)
    w_kr_pair = jnp.concatenate([-w_kr[:, half:], w_kr[:, :half]], axis=1)
    wa = jnp.concatenate([col(0), col(1), w_kr, w_kr_pair, col(5), col(6)], axis=1)
    wa = jnp.pad(wa, ((0, 0), (0, 512 - wa.shape[1])))
    wuq = mla_w_uq[l].reshape(MLA_Q_RANK, MLA_HEADS, MLA_QK)
    zq = jnp.zeros((MLA_Q_RANK, MLA_HEADS, LANES - MLA_QK), F32)
    wq_main = jnp.concatenate([wuq, zq], axis=2)
    wq_pair = jnp.concatenate([jnp.zeros((MLA_Q_RANK, MLA_HEADS, MLA_NOPE), F32), -wuq[:, :, MLA_NOPE + half:],
                               wuq[:, :, MLA_NOPE:MLA_NOPE + half], zq], axis=2)
    wq = jnp.concatenate([wq_main.reshape(MLA_Q_RANK, -1), wq_pair.reshape(MLA_Q_RANK, -1)], axis=1)
    wukv = mla_w_ukv[l].reshape(MLA_KV_RANK, MLA_HEADS, MLA_NOPE + MLA_V)
    wk_lat = jnp.concatenate([wukv[:, :, :MLA_NOPE], jnp.zeros((MLA_KV_RANK, MLA_HEADS, LANES - MLA_NOPE), F32)], axis=2)
    place = np.zeros((LANES, MLA_HEADS, LANES), np.float32)
    for r in range(MLA_ROPE):
        place[r, :, MLA_NOPE + r] = 1.0
    wk = jnp.concatenate([wk_lat.reshape(MLA_KV_RANK, -1), jnp.asarray(place).reshape(LANES, -1)], axis=0)
    wv = jnp.concatenate([wukv[:, :, MLA_NOPE:], jnp.zeros((MLA_KV_RANK, MLA_HEADS, LANES - MLA_V), F32)],
                         axis=2).reshape(MLA_KV_RANK, -1)
    vone = np.zeros((1, MLA_HEADS, LANES), np.float32)
    vone[:, :, MLA_V] = 1.0
    par = jnp.zeros((8, LANES), F32)
    par = par.at[0, MISC_A:MISC_A + DN_HEADS].set(dn_a_log[l].astype(F32))
    par = par.at[1, MISC_A:MISC_A + DN_HEADS].set(dn_dt_bias[l].astype(F32))
    if moe_w_router is None:
        wr = jnp.zeros((D_MODEL, LANES), F32)
    else:
        wr = jnp.pad(moe_w_router.astype(F32), ((0, 0), (0, LANES - N_EXPERTS)))
    wr_hi = wr.astype(BF16)
    wr_lo = jnp.concatenate([wr_hi, (wr - wr_hi.astype(F32)).astype(BF16)], axis=1)
    return dict(
        vone=jnp.asarray(vone.reshape(1, -1)), wr_hi=wr_hi, wr_lo=wr_lo,
        wa=wa.astype(BF16), wq=wq.astype(BF16), wk=wk.astype(BF16), wv=wv.astype(BF16),
        wb=col(3).astype(BF16), wc=col(4).astype(BF16), wd=jnp.concatenate([col(7), col(8)], axis=1).astype(BF16),
        qg=_row(mla_q_norm[l]), kvg=_row(mla_kv_norm[l]),
        dn_cw=jnp.pad(dn_conv_w[l].astype(F32), ((0, 8 - DN_CONV), (0, 0))), dn_par=par, dn_on=_row(dn_o_norm[l]),
        cv_w=jnp.pad(cv_dw_w[l].astype(F32), ((0, 32 - CV_WIDTH), (0, 0))), cv_b=_row(cv_dw_b[l]),
        cv_g=_row(cv_ln_g[l]), cv_be=_row(cv_ln_b[l]),
        wg=jnp.concatenate([col(9), col(10), col(11)], axis=1).astype(BF16),
        wbm=w_br_mla[l].astype(BF16), wbd=w_br_dn[l].astype(BF16), wbc=w_br_cv[l].astype(BF16),
        wmo=w_mix_out[l].astype(BF16), ln1_g=_row(ln1_g[l]), ln1_b=_row(ln1_b[l]),
        ln2_g=_row(ln2_g[l]), ln2_b=_row(ln2_b[l]),
    )


def _rope_tables(lp):
    pos = jnp.maximum(jnp.arange(lp, dtype=jnp.int32) - FRONT_PAD, 0).astype(F32)
    inv_freq = ROPE_THETA ** (-jnp.arange(0, MLA_ROPE, 2, dtype=F32) / MLA_ROPE)
    ang = pos[:, None] * inv_freq[None, :]
    cos2 = jnp.concatenate([jnp.cos(ang), jnp.cos(ang)], axis=1)
    sin2 = jnp.concatenate([jnp.sin(ang), jnp.sin(ang)], axis=1)
    scale = MLA_QK ** -0.5 * np.log2(np.e)
    zpad = jnp.zeros((lp, LANES - MLA_QK), F32)
    ct = jnp.concatenate([jnp.full((lp, MLA_NOPE), scale, F32), scale * cos2, zpad], axis=1)
    st = jnp.concatenate([jnp.zeros((lp, MLA_NOPE), F32), scale * sin2, zpad], axis=1)
    kc = jnp.concatenate([cos2, sin2, jnp.zeros((lp, LANES - 2 * MLA_ROPE), F32)], axis=1)
    return ct, st, kc


def kernel(x, meta_tokens, w_in, mla_q_norm, mla_w_uq, mla_kv_norm, mla_w_ukv, dn_conv_w, dn_a_log, dn_dt_bias, dn_o_norm, cv_dw_w, cv_dw_b, cv_ln_g, cv_ln_b, w_br_mla, w_br_dn, w_br_cv, w_mix_out, ln1_g, ln1_b, ln2_g, ln2_b, ffn_w_gate, ffn_w_up, ffn_w_down, moe_w_router, moe_w_gate, moe_w_up, moe_w_down):
    batch, seq, _ = x.shape
    assert FRONT_PAD + N_META == ROW_TILE and seq % ROW_TILE == 0 and seq >= ROW_TILE
    lp = seq_rows = ROW_TILE + seq
    tp = batch * lp
    head = jnp.concatenate([jnp.zeros((FRONT_PAD, D_MODEL), F32), meta_tokens.astype(F32)], axis=0)
    h = x.astype(F32).reshape(batch * seq, D_MODEL)
    ct, st, kc = _rope_tables(lp)
    tabs = dict(ct=jnp.tile(ct, (batch, 1)), st=jnp.tile(st, (batch, 1)), kc=jnp.tile(kc, (batch, 1)))
    for l in range(DEPTH):
        moe = l % 2 == 1
        lw = _layer_weights(l, w_in, mla_q_norm, mla_w_uq, mla_kv_norm, mla_w_ukv, dn_conv_w, dn_a_log, dn_dt_bias,
                            dn_o_norm, cv_dw_w, cv_dw_b, cv_ln_g, cv_ln_b, w_br_mla, w_br_dn, w_br_cv, w_mix_out,
                            ln1_g, ln1_b, ln2_g, ln2_b, moe_w_router[l // 2] if moe else None)
        src_head = head if l == 0 else None
        q, k, v, dn_pre, z, glu, misc = _proj_call(h, src_head, lw, tabs, tp, batch)
        o_mla = _attn_call(q, k, v, batch)
        o_dn = _dn_call(dn_pre, z, misc, lw, batch)
        h1, gates = _merge_call(h, src_head, o_mla, o_dn, glu, lw, batch, seq_rows, moe)
        if moe:
            wg, wu, wd = moe_w_gate[l // 2], moe_w_up[l // 2], moe_w_down[l // 2]
        else:
            nck = D_FF // D_FF_EXPERT
            wg = ffn_w_gate[l // 2].reshape(D_MODEL, nck, D_FF_EXPERT).transpose(1, 0, 2)
            wu = ffn_w_up[l // 2].reshape(D_MODEL, nck, D_FF_EXPERT).transpose(1, 0, 2)
            wd = ffn_w_down[l // 2].reshape(nck, D_FF_EXPERT, D_MODEL)
        h = _ffn_call(h1, gates, wg.astype(BF16), wu.astype(BF16), wd.astype(BF16), lw['ln2_g'], lw['ln2_b'],
                      batch, seq_rows, moe, l == DEPTH - 1)
    return h.reshape(batch, seq, D_MODEL).astype(x.dtype)
```

```python
import functools

import jax
import jax.numpy as jnp
import numpy as np
from jax import lax
from jax.experimental import pallas as pl
from jax.experimental.pallas import tpu as pltpu

D_MODEL = 1024
DEPTH = 2
N_META = 16
MLA_HEADS = 8
MLA_Q_RANK = 256
MLA_KV_RANK = 128
MLA_NOPE = 64
MLA_ROPE = 32
MLA_V = 64
ROPE_THETA = 10000.0
DN_HEADS = 4
DN_DK = 128
DN_DV = 128
DN_CONV = 4
CV_CH = 512
CV_WIDTH = 31
D_FF = 2816
N_EXPERTS = 8
D_FF_EXPERT = 1408
ALPHA = (2 * DEPTH) ** 0.25
EPS = 1e-6
NEG_INF = -1e30
MLA_QK = MLA_NOPE + MLA_ROPE
DN_QKV = 2 * DN_HEADS * DN_DK + DN_HEADS * DN_DV

LANES = 128
SUBLANES = 8
FRONT_PAD = 496
ROW_TILE = 512
ATT_TALL = 4
ATT_TALL_HEADS = 2
ATT_TAIL_HEADS = 4
DN_CHUNK = 128
CV_HALO = 32
DN_HALO = 8
MISC_B = 64
MISC_A = 68
VMEM_LIMIT = 56 * 1024 * 1024

BF16 = jnp.bfloat16
F32 = jnp.float32


def _dot(a, b):
    return jnp.dot(a, b, preferred_element_type=F32)


def _dot_nt(a, b):
    return lax.dot_general(a, b, (((1,), (1,)), ((), ())), preferred_element_type=F32)


def _dot_tn(a, b):
    return lax.dot_general(a, b, (((0,), (0,)), ((), ())), preferred_element_type=F32)


def _rms(x, g):
    return x * lax.rsqrt(jnp.mean(x * x, axis=-1, keepdims=True) + EPS) * g


def _layer_norm(x, g, b):
    mu = jnp.mean(x, axis=-1, keepdims=True)
    xc = x - mu
    var = jnp.mean(xc * xc, axis=-1, keepdims=True)
    return xc * lax.rsqrt(var + EPS) * g + b


def _silu(x):
    return x * jax.nn.sigmoid(x)


def _const_spec(shape):
    return pl.BlockSpec(shape, lambda *_: (0,) * len(shape))


def _frames_spec(tile, tpb):
    return pl.BlockSpec((tile, D_MODEL), lambda i, *_: ((i // tpb) * (tpb - 1) + jnp.maximum(i % tpb - 1, 0), 0))


def _load_rows(h_ref, head_ref, tiles_per_batch):
    if head_ref is None:
        return h_ref[...]
    return jnp.where(pl.program_id(0) % tiles_per_batch == 0, head_ref[...], h_ref[...])


def _proj_kernel(*refs, tiles_per_batch, from_frames):
    h_ref, head_ref = (refs[0], refs[1]) if from_frames else (refs[0], None)
    (wa_ref, wq_ref, wk_ref, wv_ref, wb_ref, wc_ref, wd_ref, qg_ref, kvg_ref, vone_ref, ct_ref, st_ref, kc_ref,
     q_ref, k_ref, v_ref, dn_ref, z_ref, glu_ref, misc_ref) = refs[2 if from_frames else 1:]
    x = _load_rows(h_ref, head_ref, tiles_per_batch).astype(BF16)
    pa = _dot(x, wa_ref[...])
    cq = pa[:, :MLA_Q_RANK]
    ckv = pa[:, MLA_Q_RANK:MLA_Q_RANK + MLA_KV_RANK]
    blk = pa[:, MLA_Q_RANK + MLA_KV_RANK:]
    cqn = _rms(cq, qg_ref[...])
    qq = _dot(cqn.astype(BF16), wq_ref[...])
    ct = ct_ref[...]
    st = st_ref[...]
    hw = MLA_HEADS * LANES
    qone = jnp.where(lax.broadcasted_iota(jnp.int32, (1, LANES), 1) == MLA_QK, 1.0, 0.0)
    for h in range(MLA_HEADS):
        sl = slice(h * LANES, (h + 1) * LANES)
        sp = slice(hw + h * LANES, hw + (h + 1) * LANES)
        q_ref[:, sl] = (qq[:, sl] * ct + qq[:, sp] * st + qone).astype(BF16)
    ckvn = _rms(ckv, kvg_ref[...])
    prod = blk * kc_ref[...]
    krr = prod + pltpu.roll(prod, LANES - MLA_ROPE, axis=1)
    kin = jnp.concatenate([ckvn, krr], axis=1).astype(BF16)
    kk = _dot(kin, wk_ref[...])
    krow = lax.broadcasted_iota(jnp.int32, kk.shape, 0)
    klane = lax.broadcasted_iota(jnp.int32, kk.shape, 1) % LANES
    padded = (pl.program_id(0) % tiles_per_batch == 0) & (krow < FRONT_PAD) & (klane == MLA_QK)
    k_ref[...] = jnp.where(padded, NEG_INF, kk).astype(BF16)
    v_ref[...] = (_dot(ckvn.astype(BF16), wv_ref[...]) + vone_ref[...]).astype(BF16)
    dn_ref[...] = _dot(x, wb_ref[...])
    z_ref[...] = _dot(x, wc_ref[...])
    cv = _dot(x, wd_ref[...])
    glu_ref[...] = cv[:, :CV_CH] * jax.nn.sigmoid(cv[:, CV_CH:])
    misc_ref[...] = blk


def _proj_call(h, head, lw, tabs, tp, batch):
    tm = ROW_TILE
    tpb = tp // batch // tm
    row = lambda w: pl.BlockSpec((tm, w), lambda i: (i, 0))
    consts = [lw['wa'], lw['wq'], lw['wk'], lw['wv'], lw['wb'], lw['wc'], lw['wd'], lw['qg'], lw['kvg'], lw['vone']]
    if head is None:
        ins, in_specs = [h], [row(D_MODEL)]
    else:
        ins, in_specs = [h, head], [_frames_spec(tm, tpb), _const_spec(head.shape)]
    ins += consts + [tabs['ct'], tabs['st'], tabs['kc']]
    in_specs += [_const_spec(a.shape) for a in consts] + [row(LANES)] * 3
    out_shape = [
        jax.ShapeDtypeStruct((tp, MLA_HEADS * LANES), BF16),
        jax.ShapeDtypeStruct((tp, MLA_HEADS * LANES), BF16),
        jax.ShapeDtypeStruct((tp, MLA_HEADS * LANES), BF16),
        jax.ShapeDtypeStruct((tp, DN_QKV), F32),
        jax.ShapeDtypeStruct((tp, DN_HEADS * DN_DV), F32),
        jax.ShapeDtypeStruct((tp, CV_CH), F32),
        jax.ShapeDtypeStruct((tp, LANES), F32),
    ]
    out_specs = [row(s.shape[1]) for s in out_shape]
    return pl.pallas_call(
        functools.partial(_proj_kernel, tiles_per_batch=tpb, from_frames=head is not None),
        grid=(tp // tm,),
        in_specs=in_specs,
        out_specs=out_specs,
        out_shape=out_shape,
        compiler_params=pltpu.CompilerParams(dimension_semantics=("parallel",), vmem_limit_bytes=VMEM_LIMIT),
        name="proj",
    )(*ins)


def _attn_kernel(it_ref, jt_ref, q_ref, ka_ref, va_ref, kb_ref, vb_ref, *rest, tile, rq, hps):
    o_ref, m_scr, acc_scr = rest[-3:]
    step = pl.program_id(2)
    i = it_ref[step]
    last = i + rq - 1
    ja = jt_ref[step]
    rows = rq * tile

    @pl.when(ja == 0)
    def _():
        m_scr[...] = jnp.full(m_scr.shape, NEG_INF, F32)
        acc_scr[...] = jnp.zeros(acc_scr.shape, F32)

    def body(r0, r1, tiles):
        keeps = []
        for _, _, jb, masked in tiles:
            if masked:
                qpos = i * tile + r0 + lax.broadcasted_iota(jnp.int32, (r1 - r0, tile), 0)
                kpos = jb * tile + lax.broadcasted_iota(jnp.int32, (r1 - r0, tile), 1)
                keeps.append((kpos <= qpos) & (kpos >= FRONT_PAD))
            else:
                keeps.append(None)
        for h in range(hps):
            sl = slice(h * LANES, (h + 1) * LANES)
            q = q_ref[r0:r1, sl]
            ss = []
            for (kr, _, _, _), keep in zip(tiles, keeps):
                s = _dot_nt(q, kr[:, sl])
                ss.append(s if keep is None else jnp.where(keep, s, NEG_INF))
            m_prev = m_scr[h, r0:r1]
            m_cur = jnp.max(ss[0], axis=1, keepdims=True)
            for s in ss[1:]:
                m_cur = jnp.maximum(m_cur, jnp.max(s, axis=1, keepdims=True))
            m_new = jnp.maximum(m_prev, m_cur)
            m_scr[h, r0:r1] = m_new
            pv = None
            for (_, vr, _, _), s in zip(tiles, ss):
                t = _dot(jnp.exp2(s - m_new[:, :1]).astype(BF16), vr[:, sl])
                pv = t if pv is None else pv + t
            acc_scr[h, r0:r1] = jnp.exp2(m_prev - m_new) * acc_scr[h, r0:r1] + pv

    ta = (ka_ref, va_ref, ja)
    tb = (kb_ref, vb_ref, ja + 1)
    if rq == 1:
        two = ja + 1 <= i
        edge = ja + 1 >= i
        pl.when(two & jnp.logical_not(edge))(lambda: body(0, rows, [ta + (False,), tb + (False,)]))
        pl.when(two & edge)(lambda: body(0, rows, [ta + (True,), tb + (True,)]))
        pl.when(jnp.logical_not(two))(lambda: body(0, rows, [ta + (True,)]))
    else:
        pl.when(ja < i)(lambda: body(0, rows, [ta + (False,), tb + (False,)]))
        for d in range(0, rq, 2):
            @pl.when(ja == i + d)
            def _(d=d):
                body(d * tile, (d + 1) * tile, [ta + (True,)])
                body((d + 1) * tile, (d + 2) * tile, [ta + (True,), tb + (True,)])
                if d + 2 < rq:
                    body((d + 2) * tile, rows, [ta + (False,), tb + (False,)])

    @pl.when(ja + 1 >= last)
    def _():
        lane = lax.broadcasted_iota(jnp.int32, (rows, LANES), 1)
        for hp in range(hps // 2):
            a0 = acc_scr[2 * hp]
            a1 = acc_scr[2 * hp + 1]
            lo = a0 / a0[:, MLA_V:MLA_V + 1]
            hi = pltpu.roll(a1 / a1[:, MLA_V:MLA_V + 1], MLA_V, axis=1)
            o_ref[:, hp * LANES:(hp + 1) * LANES] = jnp.where(lane < MLA_V, lo, hi).astype(BF16)


def _attn_call(q, k, v, batch):
    tp = q.shape[0]
    lp = tp // batch
    tile = ROW_TILE
    nb = lp // tile
    qw = MLA_HEADS * LANES
    ow = MLA_HEADS * MLA_V
    q3, k3, v3 = (a.reshape(batch, lp, qw) for a in (q, k, v))

    def call(rq, hps, q_tiles, prev):
        steps = [(i, ja) for i in q_tiles for ja in range(0, i + rq, 2)]
        it = jnp.asarray(np.array([p[0] for p in steps], np.int32))
        jt = jnp.asarray(np.array([p[1] for p in steps], np.int32))
        qmap = lambda b, g, s, it, jt: (b, it[s] // rq, g)
        first = lambda b, g, s, it, jt: (b, jt[s], g)
        second = lambda b, g, s, it, jt: (b, jnp.minimum(jt[s] + 1, it[s] + rq - 1), g)
        gw = hps * LANES
        grid_spec = pltpu.PrefetchScalarGridSpec(
            num_scalar_prefetch=2,
            grid=(batch, MLA_HEADS // hps, len(steps)),
            in_specs=[
                pl.BlockSpec((None, rq * tile, gw), qmap),
                pl.BlockSpec((None, tile, gw), first),
                pl.BlockSpec((None, tile, gw), first),
                pl.BlockSpec((None, tile, gw), second),
                pl.BlockSpec((None, tile, gw), second),
                pl.BlockSpec(memory_space=pl.ANY),
            ],
            out_specs=pl.BlockSpec((None, rq * tile, hps * MLA_V), qmap),
            scratch_shapes=[
                pltpu.VMEM((hps, rq * tile, LANES), F32),
                pltpu.VMEM((hps, rq * tile, LANES), F32),
            ],
        )
        return pl.pallas_call(
            functools.partial(_attn_kernel, tile=tile, rq=rq, hps=hps),
            grid_spec=grid_spec,
            out_shape=jax.ShapeDtypeStruct((batch, lp, ow), BF16),
            input_output_aliases={7: 0},
            compiler_params=pltpu.CompilerParams(dimension_semantics=("parallel", "parallel", "arbitrary"),
                                                 vmem_limit_bytes=VMEM_LIMIT),
            name="mla_attn_x%d" % rq,
        )(it, jt, q3, k3, v3, k3, v3, prev)

    out = jnp.zeros((batch, lp, ow), BF16)
    n_tall = nb // ATT_TALL * ATT_TALL
    if n_tall:
        out = call(ATT_TALL, ATT_TALL_HEADS, list(range(0, n_tall, ATT_TALL)), out)
    if nb > n_tall:
        out = call(1, ATT_TAIL_HEADS, list(range(n_tall, nb)), out)
    return out.reshape(tp, ow)


def _dn_items(bi, x_ref, halo_ref, misc_ref, cw_ref, par_ref, buf, tb, first, *, tile):
    buf[0:DN_HALO, :] = jnp.where(first, 0.0, halo_ref[...])
    buf[DN_HALO:DN_HALO + tile, :] = x_ref[...]
    y = cw_ref[0:1, :] * buf[pl.ds(DN_HALO - DN_CONV + 1, tile), :]
    for jj in range(1, DN_CONV):
        y = y + cw_ref[jj:jj + 1, :] * buf[pl.ds(DN_HALO - DN_CONV + 1 + jj, tile), :]
    y = _silu(y)

    misc = misc_ref[...]
    row = tb * tile + lax.broadcasted_iota(jnp.int32, (tile, LANES), 0)
    valid = row >= FRONT_PAD
    beta_all = jnp.where(valid, jax.nn.sigmoid(misc), 0.0)
    g_all = jnp.where(valid, -jnp.exp(par_ref[0:1, :]) * jax.nn.softplus(misc + par_ref[1:2, :]), 0.0)
    rin = lax.broadcasted_iota(jnp.int32, (tile, LANES), 0) % DN_CHUNK
    gc_all = g_all
    sh = 1
    while sh < DN_CHUNK:
        gc_all = gc_all + jnp.where(rin >= sh, pltpu.roll(gc_all, sh, axis=0), 0.0)
        sh *= 2
    gct_all = gc_all.T

    ii = lax.broadcasted_iota(jnp.int32, (DN_CHUNK, DN_CHUNK), 0)
    jj_ = lax.broadcasted_iota(jnp.int32, (DN_CHUNK, DN_CHUNK), 1)
    incl = ii >= jj_
    strict = ii > jj_

    nch = tile // DN_CHUNK
    items = []
    for h in range(DN_HEADS):
        hs = slice(h * DN_DK, (h + 1) * DN_DK)
        qh = y[:, hs]
        kh = y[:, DN_HEADS * DN_DK + h * DN_DK:DN_HEADS * DN_DK + (h + 1) * DN_DK]
        vh = y[:, 2 * DN_HEADS * DN_DK + h * DN_DV:2 * DN_HEADS * DN_DK + (h + 1) * DN_DV]
        qh = qh * lax.rsqrt(jnp.sum(qh * qh, axis=-1, keepdims=True) + EPS) * (DN_DK ** -0.5)
        kh = kh * lax.rsqrt(jnp.sum(kh * kh, axis=-1, keepdims=True) + EPS)
        for c in range(nch):
            rs = slice(c * DN_CHUNK, (c + 1) * DN_CHUNK)
            gcol = gc_all[rs, MISC_A + h:MISC_A + h + 1]
            grow = gct_all[MISC_A + h:MISC_A + h + 1, rs]
            bcol = beta_all[rs, MISC_B + h:MISC_B + h + 1]
            gamma = jnp.where(incl, jnp.exp(jnp.where(incl, gcol - grow, 0.0)), 0.0)
            qc = qh[rs]
            kc = kh[rs]
            kb = kc * bcol
            eg = jnp.exp(gcol)
            glast = gcol[DN_CHUNK - 1:DN_CHUNK, :]
            items.append(dict(
                b=bi, h=h, rs=rs,
                a=jnp.where(strict, _dot_nt(kb, kc) * gamma, 0.0),
                rhs=jnp.concatenate([vh[rs] * bcol, kb * eg], axis=1),
                qk=jnp.where(incl, _dot_nt(qc, kc) * gamma, 0.0),
                q_dec=qc * eg,
                k_dec=kc * jnp.exp(glast - gcol),
                cd=jnp.exp(glast),
            ))
    return items


def _dn_kernel(x_ref, halo_ref, z_ref, misc_ref, cw_ref, par_ref, on_ref, o_ref, buf, s_scr, *, tile, batch):
    tb = pl.program_id(0)
    first = tb == 0

    @pl.when(first)
    def _():
        s_scr[...] = jnp.zeros(s_scr.shape, F32)

    items = []
    for bi in range(batch):
        items += _dn_items(bi, x_ref.at[bi], halo_ref.at[bi], misc_ref.at[bi], cw_ref, par_ref, buf.at[bi], tb, first,
                           tile=tile)
    nch = tile // DN_CHUNK
    ii = lax.broadcasted_iota(jnp.int32, (DN_CHUNK, DN_CHUNK), 0)
    jj_ = lax.broadcasted_iota(jnp.int32, (DN_CHUNK, DN_CHUNK), 1)
    eye = jnp.where(ii == jj_, 1.0, 0.0).astype(F32)
    onorm = on_ref[...]
    xs = [-it['a'] for it in items]
    ts = [eye + x for x in xs]
    for _ in range(6):
        xs = [_dot(x, x) for x in xs]
        ts = [t + _dot(t, x) for t, x in zip(ts, xs)]
    for it, tinv in zip(items, ts):
        sol = _dot(tinv, it['rhs'])
        qs = _dot(it['qk'], sol)
        ks = _dot_tn(it['k_dec'], sol)
        it['ob'] = qs[:, :DN_DV]
        it['q_eff'] = it['q_dec'] - qs[:, DN_DV:]
        it['s_add'] = ks[:, :DN_DV]
        it['s_mul'] = ks[:, DN_DV:]
    states = [[s_scr[bi, h] for h in range(DN_HEADS)] for bi in range(batch)]
    for c in range(nch):
        for bi in range(batch):
            for h in range(DN_HEADS):
                it = items[(bi * DN_HEADS + h) * nch + c]
                st = states[bi][h]
                o = _dot(it['q_eff'], st) + it['ob']
                states[bi][h] = it['cd'] * st + it['s_add'] - _dot(it['s_mul'], st)
                on = _rms(o, onorm)
                zz = z_ref[bi, it['rs'], h * DN_DV:(h + 1) * DN_DV]
                o_ref[bi, it['rs'], h * DN_DV:(h + 1) * DN_DV] = (on * _silu(zz)).astype(BF16)
    for bi in range(batch):
        for h in range(DN_HEADS):
            s_scr[bi, h] = states[bi][h]


def _dn_call(dn_pre, z, misc, lw, batch):
    tp = dn_pre.shape[0]
    lp = tp // batch
    tile = ROW_TILE
    hb = tile // DN_HALO
    ow = DN_HEADS * DN_DV
    rows = lambda w: pl.BlockSpec((batch, tile, w), lambda t: (0, t, 0))
    dn3 = dn_pre.reshape(batch, lp, DN_QKV)
    out = pl.pallas_call(
        functools.partial(_dn_kernel, tile=tile, batch=batch),
        grid=(lp // tile,),
        in_specs=[
            rows(DN_QKV),
            pl.BlockSpec((batch, DN_HALO, DN_QKV), lambda t: (0, jnp.maximum(t * hb - 1, 0), 0)),
            rows(ow),
            rows(LANES),
            _const_spec(lw['dn_cw'].shape),
            _const_spec(lw['dn_par'].shape),
            _const_spec(lw['dn_on'].shape),
        ],
        out_specs=rows(ow),
        out_shape=jax.ShapeDtypeStruct((batch, lp, ow), BF16),
        scratch_shapes=[
            pltpu.VMEM((batch, DN_HALO + tile, DN_QKV), F32),
            pltpu.VMEM((batch, DN_HEADS, DN_DK, DN_DV), F32),
        ],
        compiler_params=pltpu.CompilerParams(dimension_semantics=("arbitrary",), vmem_limit_bytes=VMEM_LIMIT),
        name="deltanet",
    )(dn3, dn3, z.reshape(batch, lp, ow), misc.reshape(batch, lp, LANES), lw['dn_cw'], lw['dn_par'], lw['dn_on'])
    return out.reshape(tp, ow)


def _cv_branch(x_ref, halo_ref, w_ref, b_ref, g_ref, be_ref, buf, *, tile, tiles_per_batch):
    first = (pl.program_id(0) % tiles_per_batch) == 0
    buf[0, 0:CV_HALO, :] = jnp.where(first, 0.0, halo_ref[...])
    buf[0, CV_HALO:CV_HALO + tile, :] = x_ref[...]
    span = CV_HALO + tile - SUBLANES
    for r in range(1, SUBLANES):
        buf[r, 0:span, :] = buf[0, pl.ds(r, span), :]
    base = CV_HALO - CV_WIDTH + 1
    acc = None
    for j in range(CV_WIDTH):
        r = (base + j) % SUBLANES
        term = w_ref[j:j + 1, :] * buf[r, pl.ds(base + j - r, tile), :]
        acc = term if acc is None else acc + term
    acc = acc + b_ref[...]
    return _silu(_layer_norm(acc, g_ref[...], be_ref[...])).astype(BF16)


def _merge_kernel(*refs, tile, tiles_per_batch, seq_rows, route, from_frames):
    h_ref, head_ref = (refs[0], refs[1]) if from_frames else (refs[0], None)
    (om_ref, od_ref, glu_ref, halo_ref, cvw_ref, cvb_ref, cvg_ref, cvbe_ref, wg_ref, wbm_ref, wbd_ref, wbc_ref,
     wmo_ref, g_ref, b_ref, wrh_ref, wrl_ref, o_ref, gate_ref, cv_buf) = refs[2 if from_frames else 1:]
    hf = _load_rows(h_ref, head_ref, tiles_per_batch)
    x = hf.astype(BF16)
    o_cv = _cv_branch(glu_ref, halo_ref, cvw_ref, cvb_ref, cvg_ref, cvbe_ref, cv_buf, tile=tile,
                      tiles_per_batch=tiles_per_batch)
    merged = None
    for br, (src, wbr) in enumerate(((om_ref[...], wbm_ref), (od_ref[...], wbd_ref), (o_cv, wbc_ref))):
        gl = _dot(x, wg_ref[:, br * D_MODEL:(br + 1) * D_MODEL])
        yb = _dot(src, wbr[...])
        term = jax.nn.sigmoid(gl) * yb
        merged = term if merged is None else merged + term
    mix = _dot(merged.astype(BF16), wmo_ref[...])
    h1 = _layer_norm(ALPHA * hf + mix, g_ref[...], b_ref[...])
    row = (pl.program_id(0) % tiles_per_batch) * tile + lax.broadcasted_iota(jnp.int32, (tile, 1), 0)
    live = (row >= FRONT_PAD) & (row < seq_rows)
    h1 = jnp.where(live, h1, 0.0)
    o_ref[...] = h1
    if route:
        h1_hi = h1.astype(BF16)
        h1_lo = (h1 - h1_hi.astype(F32)).astype(BF16)
        hh = _dot(h1_hi, wrl_ref[...])
        logits = hh[:, :LANES] + hh[:, LANES:] + _dot(h1_lo, wrh_ref[...])
        lane = lax.broadcasted_iota(jnp.int32, (tile, LANES), 1)
        logits = jnp.where(lane < N_EXPERTS, logits, -jnp.inf)
        m1 = jnp.max(logits, axis=1, keepdims=True)
        i1 = jnp.min(jnp.where(logits == m1, lane, LANES), axis=1, keepdims=True)
        rest = jnp.where(lane == i1, -jnp.inf, logits)
        m2 = jnp.max(rest, axis=1, keepdims=True)
        i2 = jnp.min(jnp.where(rest == m2, lane, LANES), axis=1, keepdims=True)
        e2 = jnp.exp(m2 - m1)
        den = 1.0 + e2
        gate_ref[...] = jnp.where(lane == i1, 1.0 / den, jnp.where(lane == i2, e2 / den, 0.0))
    else:
        gate_ref[...] = jnp.ones(gate_ref.shape, F32)


def _merge_call(h, head, o_mla, o_dn, glu, lw, batch, seq_rows, route):
    tp = o_mla.shape[0]
    tile = ROW_TILE
    tpb = tp // batch // tile
    hb = tile // CV_HALO
    row = lambda w: pl.BlockSpec((tile, w), lambda i: (i, 0))
    halo = pl.BlockSpec((CV_HALO, CV_CH), lambda i: (jnp.maximum(i * hb - 1, 0), 0))
    consts = [lw['cv_w'], lw['cv_b'], lw['cv_g'], lw['cv_be'], lw['wg'], lw['wbm'], lw['wbd'], lw['wbc'], lw['wmo'],
              lw['ln1_g'], lw['ln1_b'], lw['wr_hi'], lw['wr_lo']]
    if head is None:
        ins, in_specs = [h], [row(D_MODEL)]
    else:
        ins, in_specs = [h, head], [_frames_spec(tile, tpb), _const_spec(head.shape)]
    return pl.pallas_call(
        functools.partial(_merge_kernel, tile=tile, tiles_per_batch=tpb, seq_rows=seq_rows, route=route,
                          from_frames=head is not None),
        grid=(tp // tile,),
        in_specs=in_specs + [row(512), row(512), row(CV_CH), halo] + [_const_spec(a.shape) for a in consts],
        out_specs=[row(D_MODEL), row(LANES)],
        out_shape=[jax.ShapeDtypeStruct((tp, D_MODEL), F32), jax.ShapeDtypeStruct((tp, LANES), F32)],
        scratch_shapes=[pltpu.VMEM((SUBLANES, CV_HALO + tile, CV_CH), F32)],
        compiler_params=pltpu.CompilerParams(dimension_semantics=("parallel",), vmem_limit_bytes=VMEM_LIMIT),
        name="merge_ln1",
    )(*ins, o_mla, o_dn, glu, glu, *consts)


def _ffn_kernel(h_ref, gate_ref, wg_ref, wu_ref, wd_ref, g_ref, b_ref, o_ref, acc, *, tile, tiles_per_batch, seq_rows, gated):
    c = pl.program_id(1)
    x = h_ref[...].astype(BF16)
    hid = _silu(_dot(x, wg_ref[0])) * _dot(x, wu_ref[0])
    y = _dot(hid.astype(BF16), wd_ref[0])
    if gated:
        lane = lax.broadcasted_iota(jnp.int32, (tile, LANES), 1)
        y = y * jnp.sum(jnp.where(lane == c, gate_ref[...], 0.0), axis=1, keepdims=True)

    @pl.when(c == 0)
    def _():
        acc[...] = y

    @pl.when(c > 0)
    def _():
        acc[...] = acc[...] + y

    @pl.when(c == pl.num_programs(1) - 1)
    def _():
        h2 = _layer_norm(ALPHA * h_ref[...] + acc[...], g_ref[...], b_ref[...])
        row = (pl.program_id(0) % tiles_per_batch) * tile + lax.broadcasted_iota(jnp.int32, (tile, 1), 0)
        live = (row >= FRONT_PAD) & (row < seq_rows)
        o_ref[...] = jnp.where(live, h2, 0.0)


def _ffn_call(h, gates, wg, wu, wd, ln_g, ln_b, batch, seq_rows, gated, to_frames):
    tp = h.shape[0]
    tile = ROW_TILE
    tpb = tp // batch // tile
    nchunk, _, width = wg.shape
    if to_frames:
        out_spec = _frames_spec(tile, tpb)
        out_rows = tp - batch * tile
        semantics = ("arbitrary", "arbitrary")
    else:
        out_spec = pl.BlockSpec((tile, D_MODEL), lambda i, c: (i, 0))
        out_rows = tp
        semantics = ("parallel", "arbitrary")
    return pl.pallas_call(
        functools.partial(_ffn_kernel, tile=tile, tiles_per_batch=tpb, seq_rows=seq_rows, gated=gated),
        grid=(tp // tile, nchunk),
        in_specs=[
            pl.BlockSpec((tile, D_MODEL), lambda i, c: (i, 0)),
            pl.BlockSpec((tile, LANES), lambda i, c: (i, 0)),
            pl.BlockSpec((1, D_MODEL, width), lambda i, c: (c, 0, 0)),
            pl.BlockSpec((1, D_MODEL, width), lambda i, c: (c, 0, 0)),
            pl.BlockSpec((1, width, D_MODEL), lambda i, c: (c, 0, 0)),
            _const_spec(ln_g.shape),
            _const_spec(ln_b.shape),
        ],
        out_specs=out_spec,
        out_shape=jax.ShapeDtypeStruct((out_rows, D_MODEL), F32),
        scratch_shapes=[pltpu.VMEM((tile, D_MODEL), F32)],
        compiler_params=pltpu.CompilerParams(dimension_semantics=semantics, vmem_limit_bytes=VMEM_LIMIT),
        name="ffn_ln2",
    )(h, gates, wg, wu, wd, ln_g, ln_b)


def _row(v, width=None):
    v = v.astype(F32).reshape(1, -1)
    if width is not None and v.shape[1] < width:
        v = jnp.pad(v, ((0, 0), (0, width - v.shape[1])))
    return v


def _layer_weights(l, w_in, mla_q_norm, mla_w_uq, mla_kv_norm, mla_w_ukv, dn_conv_w, dn_a_log, dn_dt_bias, dn_o_norm,
                   cv_dw_w, cv_dw_b, cv_ln_g, cv_ln_b, w_br_mla, w_br_dn, w_br_cv, w_mix_out, ln1_g, ln1_b, ln2_g, ln2_b,
                   moe_w_router):
    sizes = (MLA_Q_RANK, MLA_KV_RANK, MLA_ROPE, DN_QKV, DN_HEADS * DN_DV, DN_HEADS, DN_HEADS, CV_CH, CV_CH,
             D_MODEL, D_MODEL, D_MODEL)
    offs = np.cumsum((0,) + sizes)
    col = lambda i: w_in[l][:, offs[i]:offs[i + 1]]
    half = MLA_ROPE // 2
    w_kr = col(2)
    w_kr_pair = jnp.concatenate([-w_kr[:, half:], w_kr[:, :half]], axis=1)
    wa = jnp.concatenate([col(0), col(1), w_kr, w_kr_pair, col(5), col(6)], axis=1)
    wa = jnp.pad(wa, ((0, 0), (0, 512 - wa.shape[1])))
    wuq = mla_w_uq[l].reshape(MLA_Q_RANK, MLA_HEADS, MLA_QK)
    zq = jnp.zeros((MLA_Q_RANK, MLA_HEADS, LANES - MLA_QK), F32)
    wq_main = jnp.concatenate([wuq, zq], axis=2)
    wq_pair = jnp.concatenate([jnp.zeros((MLA_Q_RANK, MLA_HEADS, MLA_NOPE), F32), -wuq[:, :, MLA_NOPE + half:],
                               wuq[:, :, MLA_NOPE:MLA_NOPE + half], zq], axis=2)
    wq = jnp.concatenate([wq_main.reshape(MLA_Q_RANK, -1), wq_pair.reshape(MLA_Q_RANK, -1)], axis=1)
    wukv = mla_w_ukv[l].reshape(MLA_KV_RANK, MLA_HEADS, MLA_NOPE + MLA_V)
    wk_lat = jnp.concatenate([wukv[:, :, :MLA_NOPE], jnp.zeros((MLA_KV_RANK, MLA_HEADS, LANES - MLA_NOPE), F32)], axis=2)
    place = np.zeros((LANES, MLA_HEADS, LANES), np.float32)
    for r in range(MLA_ROPE):
        place[r, :, MLA_NOPE + r] = 1.0
    wk = jnp.concatenate([wk_lat.reshape(MLA_KV_RANK, -1), jnp.asarray(place).reshape(LANES, -1)], axis=0)
    wv = jnp.concatenate([wukv[:, :, MLA_NOPE:], jnp.zeros((MLA_KV_RANK, MLA_HEADS, LANES - MLA_V), F32)],
                         axis=2).reshape(MLA_KV_RANK, -1)
    vone = np.zeros((1, MLA_HEADS, LANES), np.float32)
    vone[:, :, MLA_V] = 1.0
    par = jnp.zeros((8, LANES), F32)
    par = par.at[0, MISC_A:MISC_A + DN_HEADS].set(dn_a_log[l].astype(F32))
    par = par.at[1, MISC_A:MISC_A + DN_HEADS].set(dn_dt_bias[l].astype(F32))
    if moe_w_router is None:
        wr = jnp.zeros((D_MODEL, LANES), F32)
    else:
        wr = jnp.pad(moe_w_router.astype(F32), ((0, 0), (0, LANES - N_EXPERTS)))
    wr_hi = wr.astype(BF16)
    wr_lo = jnp.concatenate([wr_hi, (wr - wr_hi.astype(F32)).astype(BF16)], axis=1)
    return dict(
        vone=jnp.asarray(vone.reshape(1, -1)), wr_hi=wr_hi, wr_lo=wr_lo,
        wa=wa.astype(BF16), wq=wq.astype(BF16), wk=wk.astype(BF16), wv=wv.astype(BF16),
        wb=col(3).astype(BF16), wc=col(4).astype(BF16), wd=jnp.concatenate([col(7), col(8)], axis=1).astype(BF16),
        qg=_row(mla_q_norm[l]), kvg=_row(mla_kv_norm[l]),
        dn_cw=jnp.pad(dn_conv_w[l].astype(F32), ((0, 8 - DN_CONV), (0, 0))), dn_par=par, dn_on=_row(dn_o_norm[l]),
        cv_w=jnp.pad(cv_dw_w[l].astype(F32), ((0, 32 - CV_WIDTH), (0, 0))), cv_b=_row(cv_dw_b[l]),
        cv_g=_row(cv_ln_g[l]), cv_be=_row(cv_ln_b[l]),
        wg=jnp.concatenate([col(9), col(10), col(11)], axis=1).astype(BF16),
        wbm=w_br_mla[l].astype(BF16), wbd=w_br_dn[l].astype(BF16), wbc=w_br_cv[l].astype(BF16),
        wmo=w_mix_out[l].astype(BF16), ln1_g=_row(ln1_g[l]), ln1_b=_row(ln1_b[l]),
        ln2_g=_row(ln2_g[l]), ln2_b=_row(ln2_b[l]),
    )


def _rope_tables(lp):
    pos = jnp.maximum(jnp.arange(lp, dtype=jnp.int32) - FRONT_PAD, 0).astype(F32)
    inv_freq = ROPE_THETA ** (-jnp.arange(0, MLA_ROPE, 2, dtype=F32) / MLA_ROPE)
    ang = pos[:, None] * inv_freq[None, :]
    cos2 = jnp.concatenate([jnp.cos(ang), jnp.cos(ang)], axis=1)
    sin2 = jnp.concatenate([jnp.sin(ang), jnp.sin(ang)], axis=1)
    scale = MLA_QK ** -0.5 * np.log2(np.e)
    zpad = jnp.zeros((lp, LANES - MLA_QK), F32)
    ct = jnp.concatenate([jnp.full((lp, MLA_NOPE), scale, F32), scale * cos2, zpad], axis=1)
    st = jnp.concatenate([jnp.zeros((lp, MLA_NOPE), F32), scale * sin2, zpad], axis=1)
    kc = jnp.concatenate([cos2, sin2, jnp.zeros((lp, LANES - 2 * MLA_ROPE), F32)], axis=1)
    return ct, st, kc


def kernel(x, meta_tokens, w_in, mla_q_norm, mla_w_uq, mla_kv_norm, mla_w_ukv, dn_conv_w, dn_a_log, dn_dt_bias, dn_o_norm, cv_dw_w, cv_dw_b, cv_ln_g, cv_ln_b, w_br_mla, w_br_dn, w_br_cv, w_mix_out, ln1_g, ln1_b, ln2_g, ln2_b, ffn_w_gate, ffn_w_up, ffn_w_down, moe_w_router, moe_w_gate, moe_w_up, moe_w_down):
    batch, seq, _ = x.shape
    assert FRONT_PAD + N_META == ROW_TILE and seq % ROW_TILE == 0 and seq >= ROW_TILE
    lp = seq_rows = ROW_TILE + seq
    tp = batch * lp
    head = jnp.concatenate([jnp.zeros((FRONT_PAD, D_MODEL), F32), meta_tokens.astype(F32)], axis=0)
    h = x.astype(F32).reshape(batch * seq, D_MODEL)
    ct, st, kc = _rope_tables(lp)
    tabs = dict(ct=jnp.tile(ct, (batch, 1)), st=jnp.tile(st, (batch, 1)), kc=jnp.tile(kc, (batch, 1)))
    for l in range(DEPTH):
        moe = l % 2 == 1
        lw = _layer_weights(l, w_in, mla_q_norm, mla_w_uq, mla_kv_norm, mla_w_ukv, dn_conv_w, dn_a_log, dn_dt_bias,
                            dn_o_norm, cv_dw_w, cv_dw_b, cv_ln_g, cv_ln_b, w_br_mla, w_br_dn, w_br_cv, w_mix_out,
                            ln1_g, ln1_b, ln2_g, ln2_b, moe_w_router[l // 2] if moe else None)
        src_head = head if l == 0 else None
        q, k, v, dn_pre, z, glu, misc = _proj_call(h, src_head, lw, tabs, tp, batch)
        o_mla = _attn_call(q, k, v, batch)
        o_dn = _dn_call(dn_pre, z, misc, lw, batch)
        h1, gates = _merge_call(h, src_head, o_mla, o_dn, glu, lw, batch, seq_rows, moe)
        if moe:
            wg, wu, wd = moe_w_gate[l // 2], moe_w_up[l // 2], moe_w_down[l // 2]
        else:
            nck = D_FF // D_FF_EXPERT
            wg = ffn_w_gate[l // 2].reshape(D_MODEL, nck, D_FF_EXPERT).transpose(1, 0, 2)
            wu = ffn_w_up[l // 2].reshape(D_MODEL, nck, D_FF_EXPERT).transpose(1, 0, 2)
            wd = ffn_w_down[l // 2].reshape(nck, D_FF_EXPERT, D_MODEL)
        h = _ffn_call(h1, gates, wg.astype(BF16), wu.astype(BF16), wd.astype(BF16), lw['ln2_g'], lw['ln2_b'],
                      batch, seq_rows, moe, l == DEPTH - 1)
    return h.reshape(batch, seq, D_MODEL).astype(x.dtype)
```

```python
import functools

import jax
import jax.numpy as jnp
import numpy as np
from jax import lax
from jax.experimental import pallas as pl
from jax.experimental.pallas import tpu as pltpu

D_MODEL = 1024
DEPTH = 2
N_META = 16
MLA_HEADS = 8
MLA_Q_RANK = 256
MLA_KV_RANK = 128
MLA_NOPE = 64
MLA_ROPE = 32
MLA_V = 64
ROPE_THETA = 10000.0
DN_HEADS = 4
DN_DK = 128
DN_DV = 128
DN_CONV = 4
CV_CH = 512
CV_WIDTH = 31
D_FF = 2816
N_EXPERTS = 8
D_FF_EXPERT = 1408
ALPHA = (2 * DEPTH) ** 0.25
EPS = 1e-6
NEG_INF = -1e30
MLA_QK = MLA_NOPE + MLA_ROPE
DN_QKV = 2 * DN_HEADS * DN_DK + DN_HEADS * DN_DV

LANES = 128
SUBLANES = 8
FRONT_PAD = 496
ROW_TILE = 512
ATT_TALL = 4
ATT_TALL_HEADS = 2
ATT_TAIL_HEADS = 4
DN_CHUNK = 128
CV_HALO = 32
DN_HALO = 8
MISC_B = 64
MISC_A = 68
VMEM_LIMIT = 56 * 1024 * 1024

BF16 = jnp.bfloat16
F32 = jnp.float32


def _dot(a, b):
    return jnp.dot(a, b, preferred_element_type=F32)


def _dot_nt(a, b):
    return lax.dot_general(a, b, (((1,), (1,)), ((), ())), preferred_element_type=F32)


def _dot_tn(a, b):
    return lax.dot_general(a, b, (((0,), (0,)), ((), ())), preferred_element_type=F32)


def _rms(x, g):
    return x * lax.rsqrt(jnp.mean(x * x, axis=-1, keepdims=True) + EPS) * g


def _layer_norm(x, g, b):
    mu = jnp.mean(x, axis=-1, keepdims=True)
    xc = x - mu
    var = jnp.mean(xc * xc, axis=-1, keepdims=True)
    return xc * lax.rsqrt(var + EPS) * g + b


def _silu(x):
    return x * jax.nn.sigmoid(x)


def _const_spec(shape):
    return pl.BlockSpec(shape, lambda *_: (0,) * len(shape))


def _frames_spec(tile, tpb):
    return pl.BlockSpec((tile, D_MODEL), lambda i, *_: ((i // tpb) * (tpb - 1) + jnp.maximum(i % tpb - 1, 0), 0))


def _load_rows(h_ref, head_ref, tiles_per_batch):
    if head_ref is None:
        return h_ref[...]
    return jnp.where(pl.program_id(0) % tiles_per_batch == 0, head_ref[...], h_ref[...])


def _proj_kernel(*refs, tiles_per_batch, from_frames):
    h_ref, head_ref = (refs[0], refs[1]) if from_frames else (refs[0], None)
    (wa_ref, wq_ref, wk_ref, wv_ref, wb_ref, wc_ref, wd_ref, qg_ref, kvg_ref, vone_ref, ct_ref, st_ref, kc_ref,
     q_ref, k_ref, v_ref, dn_ref, z_ref, glu_ref, misc_ref) = refs[2 if from_frames else 1:]
    x = _load_rows(h_ref, head_ref, tiles_per_batch).astype(BF16)
    pa = _dot(x, wa_ref[...])
    cq = pa[:, :MLA_Q_RANK]
    ckv = pa[:, MLA_Q_RANK:MLA_Q_RANK + MLA_KV_RANK]
    blk = pa[:, MLA_Q_RANK + MLA_KV_RANK:]
    cqn = _rms(cq, qg_ref[...])
    qq = _dot(cqn.astype(BF16), wq_ref[...])
    ct = ct_ref[...]
    st = st_ref[...]
    hw = MLA_HEADS * LANES
    qone = jnp.where(lax.broadcasted_iota(jnp.int32, (1, LANES), 1) == MLA_QK, 1.0, 0.0)
    for h in range(MLA_HEADS):
        sl = slice(h * LANES, (h + 1) * LANES)
        sp = slice(hw + h * LANES, hw + (h + 1) * LANES)
        q_ref[:, sl] = (qq[:, sl] * ct + qq[:, sp] * st + qone).astype(BF16)
    ckvn = _rms(ckv, kvg_ref[...])
    prod = blk * kc_ref[...]
    krr = prod + pltpu.roll(prod, LANES - MLA_ROPE, axis=1)
    kin = jnp.concatenate([ckvn, krr], axis=1).astype(BF16)
    kk = _dot(kin, wk_ref[...])
    krow = lax.broadcasted_iota(jnp.int32, kk.shape, 0)
    klane = lax.broadcasted_iota(jnp.int32, kk.shape, 1) % LANES
    padded = (pl.program_id(0) % tiles_per_batch == 0) & (krow < FRONT_PAD) & (klane == MLA_QK)
    k_ref[...] = jnp.where(padded, NEG_INF, kk).astype(BF16)
    v_ref[...] = (_dot(ckvn.astype(BF16), wv_ref[...]) + vone_ref[...]).astype(BF16)
    dn_ref[...] = _dot(x, wb_ref[...])
    z_ref[...] = _dot(x, wc_ref[...])
    cv = _dot(x, wd_ref[...])
    glu_ref[...] = cv[:, :CV_CH] * jax.nn.sigmoid(cv[:, CV_CH:])
    misc_ref[...] = blk


def _proj_call(h, head, lw, tabs, tp, batch):
    tm = ROW_TILE
    tpb = tp // batch // tm
    row = lambda w: pl.BlockSpec((tm, w), lambda i: (i, 0))
    consts = [lw['wa'], lw['wq'], lw['wk'], lw['wv'], lw['wb'], lw['wc'], lw['wd'], lw['qg'], lw['kvg'], lw['vone']]
    if head is None:
        ins, in_specs = [h], [row(D_MODEL)]
    else:
        ins, in_specs = [h, head], [_frames_spec(tm, tpb), _const_spec(head.shape)]
    ins += consts + [tabs['ct'], tabs['st'], tabs['kc']]
    in_specs += [_const_spec(a.shape) for a in consts] + [row(LANES)] * 3
    out_shape = [
        jax.ShapeDtypeStruct((tp, MLA_HEADS * LANES), BF16),
        jax.ShapeDtypeStruct((tp, MLA_HEADS * LANES), BF16),
        jax.ShapeDtypeStruct((tp, MLA_HEADS * LANES), BF16),
        jax.ShapeDtypeStruct((tp, DN_QKV), F32),
        jax.ShapeDtypeStruct((tp, DN_HEADS * DN_DV), F32),
        jax.ShapeDtypeStruct((tp, CV_CH), F32),
        jax.ShapeDtypeStruct((tp, LANES), F32),
    ]
    out_specs = [row(s.shape[1]) for s in out_shape]
    return pl.pallas_call(
        functools.partial(_proj_kernel, tiles_per_batch=tpb, from_frames=head is not None),
        grid=(tp // tm,),
        in_specs=in_specs,
        out_specs=out_specs,
        out_shape=out_shape,
        compiler_params=pltpu.CompilerParams(dimension_semantics=("parallel",), vmem_limit_bytes=VMEM_LIMIT),
        name="proj",
    )(*ins)


def _attn_kernel(it_ref, jt_ref, q_ref, ka_ref, va_ref, kb_ref, vb_ref, *rest, tile, rq, hps):
    o_ref, m_scr, acc_scr = rest[-3:]
    step = pl.program_id(2)
    i = it_ref[step]
    last = i + rq - 1
    ja = jt_ref[step]
    rows = rq * tile

    @pl.when(ja == 0)
    def _():
        m_scr[...] = jnp.full(m_scr.shape, NEG_INF, F32)
        acc_scr[...] = jnp.zeros(acc_scr.shape, F32)

    def body(r0, r1, tiles):
        keeps = []
        for _, _, jb, masked in tiles:
            if masked:
                qpos = i * tile + r0 + lax.broadcasted_iota(jnp.int32, (r1 - r0, tile), 0)
                kpos = jb * tile + lax.broadcasted_iota(jnp.int32, (r1 - r0, tile), 1)
                keeps.append((kpos <= qpos) & (kpos >= FRONT_PAD))
            else:
                keeps.append(None)
        for h in range(hps):
            sl = slice(h * LANES, (h + 1) * LANES)
            q = q_ref[r0:r1, sl]
            ss = []
            for (kr, _, _, _), keep in zip(tiles, keeps):
                s = _dot_nt(q, kr[:, sl])
                ss.append(s if keep is None else jnp.where(keep, s, NEG_INF))
            m_prev = m_scr[h, r0:r1]
            m_cur = jnp.max(ss[0], axis=1, keepdims=True)
            for s in ss[1:]:
                m_cur = jnp.maximum(m_cur, jnp.max(s, axis=1, keepdims=True))
            m_new = jnp.maximum(m_prev, m_cur)
            m_scr[h, r0:r1] = m_new
            pv = None
            for (_, vr, _, _), s in zip(tiles, ss):
                t = _dot(jnp.exp2(s - m_new[:, :1]).astype(BF16), vr[:, sl])
                pv = t if pv is None else pv + t
            acc_scr[h, r0:r1] = jnp.exp2(m_prev - m_new) * acc_scr[h, r0:r1] + pv

    ta = (ka_ref, va_ref, ja)
    tb = (kb_ref, vb_ref, ja + 1)
    if rq == 1:
        two = ja + 1 <= i
        edge = ja + 1 >= i
        pl.when(two & jnp.logical_not(edge))(lambda: body(0, rows, [ta + (False,), tb + (False,)]))
        pl.when(two & edge)(lambda: body(0, rows, [ta + (True,), tb + (True,)]))
        pl.when(jnp.logical_not(two))(lambda: body(0, rows, [ta + (True,)]))
    else:
        pl.when(ja < i)(lambda: body(0, rows, [ta + (False,), tb + (False,)]))
        for d in range(0, rq, 2):
            @pl.when(ja == i + d)
            def _(d=d):
                body(d * tile, (d + 1) * tile, [ta + (True,)])
                body((d + 1) * tile, (d + 2) * tile, [ta + (True,), tb + (True,)])
                if d + 2 < rq:
                    body((d + 2) * tile, rows, [ta + (False,), tb + (False,)])

    @pl.when(ja + 1 >= last)
    def _():
        lane = lax.broadcasted_iota(jnp.int32, (rows, LANES), 1)
        for hp in range(hps // 2):
            a0 = acc_scr[2 * hp]
            a1 = acc_scr[2 * hp + 1]
            lo = a0 / a0[:, MLA_V:MLA_V + 1]
            hi = pltpu.roll(a1 / a1[:, MLA_V:MLA_V + 1], MLA_V, axis=1)
            o_ref[:, hp * LANES:(hp + 1) * LANES] = jnp.where(lane < MLA_V, lo, hi).astype(BF16)


def _attn_call(q, k, v, batch):
    tp = q.shape[0]
    lp = tp // batch
    tile = ROW_TILE
    nb = lp // tile
    qw = MLA_HEADS * LANES
    ow = MLA_HEADS * MLA_V
    q3, k3, v3 = (a.reshape(batch, lp, qw) for a in (q, k, v))

    def call(rq, hps, q_tiles, prev):
        steps = [(i, ja) for i in q_tiles for ja in range(0, i + rq, 2)]
        it = jnp.asarray(np.array([p[0] for p in steps], np.int32))
        jt = jnp.asarray(np.array([p[1] for p in steps], np.int32))
        qmap = lambda b, g, s, it, jt: (b, it[s] // rq, g)
        first = lambda b, g, s, it, jt: (b, jt[s], g)
        second = lambda b, g, s, it, jt: (b, jnp.minimum(jt[s] + 1, it[s] + rq - 1), g)
        gw = hps * LANES
        grid_spec = pltpu.PrefetchScalarGridSpec(
            num_scalar_prefetch=2,
            grid=(batch, MLA_HEADS // hps, len(steps)),
            in_specs=[
                pl.BlockSpec((None, rq * tile, gw), qmap),
                pl.BlockSpec((None, tile, gw), first),
                pl.BlockSpec((None, tile, gw), first),
                pl.BlockSpec((None, tile, gw), second),
                pl.BlockSpec((None, tile, gw), second),
                pl.BlockSpec(memory_space=pl.ANY),
            ],
            out_specs=pl.BlockSpec((None, rq * tile, hps * MLA_V), qmap),
            scratch_shapes=[
                pltpu.VMEM((hps, rq * tile, LANES), F32),
                pltpu.VMEM((hps, rq * tile, LANES), F32),
            ],
        )
        return pl.pallas_call(
            functools.partial(_attn_kernel, tile=tile, rq=rq, hps=hps),
            grid_spec=grid_spec,
            out_shape=jax.ShapeDtypeStruct((batch, lp, ow), BF16),
            input_output_aliases={7: 0},
            compiler_params=pltpu.CompilerParams(dimension_semantics=("parallel", "parallel", "arbitrary"),
                                                 vmem_limit_bytes=VMEM_LIMIT),
            name="mla_attn_x%d" % rq,
        )(it, jt, q3, k3, v3, k3, v3, prev)

    out = jnp.zeros((batch, lp, ow), BF16)
    n_tall = nb // ATT_TALL * ATT_TALL
    if n_tall:
        out = call(ATT_TALL, ATT_TALL_HEADS, list(range(0, n_tall, ATT_TALL)), out)
    if nb > n_tall:
        out = call(1, ATT_TAIL_HEADS, list(range(n_tall, nb)), out)
    return out.reshape(tp, ow)


def _dn_items(bi, x_ref, halo_ref, misc_ref, cw_ref, par_ref, buf, tb, first, *, tile):
    buf[0:DN_HALO, :] = jnp.where(first, 0.0, halo_ref[...])
    buf[DN_HALO:DN_HALO + tile, :] = x_ref[...]
    y = cw_ref[0:1, :] * buf[pl.ds(DN_HALO - DN_CONV + 1, tile), :]
    for jj in range(1, DN_CONV):
        y = y + cw_ref[jj:jj + 1, :] * buf[pl.ds(DN_HALO - DN_CONV + 1 + jj, tile), :]
    y = _silu(y)

    misc = misc_ref[...]
    row = tb * tile + lax.broadcasted_iota(jnp.int32, (tile, LANES), 0)
    valid = row >= FRONT_PAD
    beta_all = jnp.where(valid, jax.nn.sigmoid(misc), 0.0)
    g_all = jnp.where(valid, -jnp.exp(par_ref[0:1, :]) * jax.nn.softplus(misc + par_ref[1:2, :]), 0.0)
    rin = lax.broadcasted_iota(jnp.int32, (tile, LANES), 0) % DN_CHUNK
    gc_all = g_all
    sh = 1
    while sh < DN_CHUNK:
        gc_all = gc_all + jnp.where(rin >= sh, pltpu.roll(gc_all, sh, axis=0), 0.0)
        sh *= 2
    gct_all = gc_all.T

    ii = lax.broadcasted_iota(jnp.int32, (DN_CHUNK, DN_CHUNK), 0)
    jj_ = lax.broadcasted_iota(jnp.int32, (DN_CHUNK, DN_CHUNK), 1)
    incl = ii >= jj_
    strict = ii > jj_

    nch = tile // DN_CHUNK
    items = []
    for h in range(DN_HEADS):
        hs = slice(h * DN_DK, (h + 1) * DN_DK)
        qh = y[:, hs]
        kh = y[:, DN_HEADS * DN_DK + h * DN_DK:DN_HEADS * DN_DK + (h + 1) * DN_DK]
        vh = y[:, 2 * DN_HEADS * DN_DK + h * DN_DV:2 * DN_HEADS * DN_DK + (h + 1) * DN_DV]
        qh = qh * lax.rsqrt(jnp.sum(qh * qh, axis=-1, keepdims=True) + EPS) * (DN_DK ** -0.5)
        kh = kh * lax.rsqrt(jnp.sum(kh * kh, axis=-1, keepdims=True) + EPS)
        for c in range(nch):
            rs = slice(c * DN_CHUNK, (c + 1) * DN_CHUNK)
            gcol = gc_all[rs, MISC_A + h:MISC_A + h + 1]
            grow = gct_all[MISC_A + h:MISC_A + h + 1, rs]
            bcol = beta_all[rs, MISC_B + h:MISC_B + h + 1]
            gamma = jnp.where(incl, jnp.exp(jnp.where(incl, gcol - grow, 0.0)), 0.0)
            qc = qh[rs]
            kc = kh[rs]
            kb = kc * bcol
            eg = jnp.exp(gcol)
            glast = gcol[DN_CHUNK - 1:DN_CHUNK, :]
            items.append(dict(
                b=bi, h=h, rs=rs,
                a=jnp.where(strict, _dot_nt(kb, kc) * gamma, 0.0),
                rhs=jnp.concatenate([vh[rs] * bcol, kb * eg], axis=1),
                qk=jnp.where(incl, _dot_nt(qc, kc) * gamma, 0.0),
                q_dec=qc * eg,
                k_dec=kc * jnp.exp(glast - gcol),
                cd=jnp.exp(glast),
            ))
    return items


def _dn_kernel(x_ref, halo_ref, z_ref, misc_ref, cw_ref, par_ref, on_ref, o_ref, buf, s_scr, *, tile, batch):
    tb = pl.program_id(0)
    first = tb == 0

    @pl.when(first)
    def _():
        s_scr[...] = jnp.zeros(s_scr.shape, F32)

    items = []
    for bi in range(batch):
        items += _dn_items(bi, x_ref.at[bi], halo_ref.at[bi], misc_ref.at[bi], cw_ref, par_ref, buf.at[bi], tb, first,
                           tile=tile)
    nch = tile // DN_CHUNK
    ii = lax.broadcasted_iota(jnp.int32, (DN_CHUNK, DN_CHUNK), 0)
    jj_ = lax.broadcasted_iota(jnp.int32, (DN_CHUNK, DN_CHUNK), 1)
    eye = jnp.where(ii == jj_, 1.0, 0.0).astype(F32)
    onorm = on_ref[...]
    xs = [-it['a'] for it in items]
    ts = [eye + x for x in xs]
    for _ in range(6):
        xb = [x.astype(BF16) for x in xs]
        xs = [_dot(b, b) for b in xb]
        ts = [t + _dot(t.astype(BF16), x.astype(BF16)) for t, x in zip(ts, xs)]
    for it, tinv in zip(items, ts):
        sol = _dot(tinv, it['rhs'])
        qs = _dot(it['qk'], sol)
        ks = _dot_tn(it['k_dec'], sol)
        it['ob'] = qs[:, :DN_DV]
        it['q_eff'] = it['q_dec'] - qs[:, DN_DV:]
        it['s_add'] = ks[:, :DN_DV]
        it['s_mul'] = ks[:, DN_DV:]
    states = [[s_scr[bi, h] for h in range(DN_HEADS)] for bi in range(batch)]
    for c in range(nch):
        for bi in range(batch):
            for h in range(DN_HEADS):
                it = items[(bi * DN_HEADS + h) * nch + c]
                st = states[bi][h]
                o = _dot(it['q_eff'], st) + it['ob']
                states[bi][h] = it['cd'] * st + it['s_add'] - _dot(it['s_mul'], st)
                on = _rms(o, onorm)
                zz = z_ref[bi, it['rs'], h * DN_DV:(h + 1) * DN_DV]
                o_ref[bi, it['rs'], h * DN_DV:(h + 1) * DN_DV] = (on * _silu(zz)).astype(BF16)
    for bi in range(batch):
        for h in range(DN_HEADS):
            s_scr[bi, h] = states[bi][h]


def _dn_call(dn_pre, z, misc, lw, batch):
    tp = dn_pre.shape[0]
    lp = tp // batch
    tile = ROW_TILE
    hb = tile // DN_HALO
    ow = DN_HEADS * DN_DV
    rows = lambda w: pl.BlockSpec((batch, tile, w), lambda t: (0, t, 0))
    dn3 = dn_pre.reshape(batch, lp, DN_QKV)
    out = pl.pallas_call(
        functools.partial(_dn_kernel, tile=tile, batch=batch),
        grid=(lp // tile,),
        in_specs=[
            rows(DN_QKV),
            pl.BlockSpec((batch, DN_HALO, DN_QKV), lambda t: (0, jnp.maximum(t * hb - 1, 0), 0)),
            rows(ow),
            rows(LANES),
            _const_spec(lw['dn_cw'].shape),
            _const_spec(lw['dn_par'].shape),
            _const_spec(lw['dn_on'].shape),
        ],
        out_specs=rows(ow),
        out_shape=jax.ShapeDtypeStruct((batch, lp, ow), BF16),
        scratch_shapes=[
            pltpu.VMEM((batch, DN_HALO + tile, DN_QKV), F32),
            pltpu.VMEM((batch, DN_HEADS, DN_DK, DN_DV), F32),
        ],
        compiler_params=pltpu.CompilerParams(dimension_semantics=("arbitrary",), vmem_limit_bytes=VMEM_LIMIT),
        name="deltanet",
    )(dn3, dn3, z.reshape(batch, lp, ow), misc.reshape(batch, lp, LANES), lw['dn_cw'], lw['dn_par'], lw['dn_on'])
    return out.reshape(tp, ow)


def _cv_branch(x_ref, halo_ref, w_ref, b_ref, g_ref, be_ref, buf, *, tile, tiles_per_batch):
    first = (pl.program_id(0) % tiles_per_batch) == 0
    buf[0, 0:CV_HALO, :] = jnp.where(first, 0.0, halo_ref[...])
    buf[0, CV_HALO:CV_HALO + tile, :] = x_ref[...]
    span = CV_HALO + tile - SUBLANES
    for r in range(1, SUBLANES):
        buf[r, 0:span, :] = buf[0, pl.ds(r, span), :]
    base = CV_HALO - CV_WIDTH + 1
    acc = None
    for j in range(CV_WIDTH):
        r = (base + j) % SUBLANES
        term = w_ref[j:j + 1, :] * buf[r, pl.ds(base + j - r, tile), :]
        acc = term if acc is None else acc + term
    acc = acc + b_ref[...]
    return _silu(_layer_norm(acc, g_ref[...], be_ref[...])).astype(BF16)


def _merge_kernel(*refs, tile, tiles_per_batch, seq_rows, route, from_frames):
    h_ref, head_ref = (refs[0], refs[1]) if from_frames else (refs[0], None)
    (om_ref, od_ref, glu_ref, halo_ref, cvw_ref, cvb_ref, cvg_ref, cvbe_ref, wg_ref, wbm_ref, wbd_ref, wbc_ref,
     wmo_ref, g_ref, b_ref, wrh_ref, wrl_ref, o_ref, gate_ref, cv_buf) = refs[2 if from_frames else 1:]
    hf = _load_rows(h_ref, head_ref, tiles_per_batch)
    x = hf.astype(BF16)
    o_cv = _cv_branch(glu_ref, halo_ref, cvw_ref, cvb_ref, cvg_ref, cvbe_ref, cv_buf, tile=tile,
                      tiles_per_batch=tiles_per_batch)
    merged = None
    for br, (src, wbr) in enumerate(((om_ref[...], wbm_ref), (od_ref[...], wbd_ref), (o_cv, wbc_ref))):
        gl = _dot(x, wg_ref[:, br * D_MODEL:(br + 1) * D_MODEL])
        yb = _dot(src, wbr[...])
        term = jax.nn.sigmoid(gl) * yb
        merged = term if merged is None else merged + term
    mix = _dot(merged.astype(BF16), wmo_ref[...])
    h1 = _layer_norm(ALPHA * hf + mix, g_ref[...], b_ref[...])
    row = (pl.program_id(0) % tiles_per_batch) * tile + lax.broadcasted_iota(jnp.int32, (tile, 1), 0)
    live = (row >= FRONT_PAD) & (row < seq_rows)
    h1 = jnp.where(live, h1, 0.0)
    o_ref[...] = h1
    if route:
        h1_hi = h1.astype(BF16)
        h1_lo = (h1 - h1_hi.astype(F32)).astype(BF16)
        hh = _dot(h1_hi, wrl_ref[...])
        logits = hh[:, :LANES] + hh[:, LANES:] + _dot(h1_lo, wrh_ref[...])
        lane = lax.broadcasted_iota(jnp.int32, (tile, LANES), 1)
        logits = jnp.where(lane < N_EXPERTS, logits, -jnp.inf)
        m1 = jnp.max(logits, axis=1, keepdims=True)
        i1 = jnp.min(jnp.where(logits == m1, lane, LANES), axis=1, keepdims=True)
        rest = jnp.where(lane == i1, -jnp.inf, logits)
        m2 = jnp.max(rest, axis=1, keepdims=True)
        i2 = jnp.min(jnp.where(rest == m2, lane, LANES), axis=1, keepdims=True)
        e2 = jnp.exp(m2 - m1)
        den = 1.0 + e2
        gate_ref[...] = jnp.where(lane == i1, 1.0 / den, jnp.where(lane == i2, e2 / den, 0.0))
    else:
        gate_ref[...] = jnp.ones(gate_ref.shape, F32)


def _merge_call(h, head, o_mla, o_dn, glu, lw, batch, seq_rows, route):
    tp = o_mla.shape[0]
    tile = ROW_TILE
    tpb = tp // batch // tile
    hb = tile // CV_HALO
    row = lambda w: pl.BlockSpec((tile, w), lambda i: (i, 0))
    halo = pl.BlockSpec((CV_HALO, CV_CH), lambda i: (jnp.maximum(i * hb - 1, 0), 0))
    consts = [lw['cv_w'], lw['cv_b'], lw['cv_g'], lw['cv_be'], lw['wg'], lw['wbm'], lw['wbd'], lw['wbc'], lw['wmo'],
              lw['ln1_g'], lw['ln1_b'], lw['wr_hi'], lw['wr_lo']]
    if head is None:
        ins, in_specs = [h], [row(D_MODEL)]
    else:
        ins, in_specs = [h, head], [_frames_spec(tile, tpb), _const_spec(head.shape)]
    return pl.pallas_call(
        functools.partial(_merge_kernel, tile=tile, tiles_per_batch=tpb, seq_rows=seq_rows, route=route,
                          from_frames=head is not None),
        grid=(tp // tile,),
        in_specs=in_specs + [row(512), row(512), row(CV_CH), halo] + [_const_spec(a.shape) for a in consts],
        out_specs=[row(D_MODEL), row(LANES)],
        out_shape=[jax.ShapeDtypeStruct((tp, D_MODEL), F32), jax.ShapeDtypeStruct((tp, LANES), F32)],
        scratch_shapes=[pltpu.VMEM((SUBLANES, CV_HALO + tile, CV_CH), F32)],
        compiler_params=pltpu.CompilerParams(dimension_semantics=("parallel",), vmem_limit_bytes=VMEM_LIMIT),
        name="merge_ln1",
    )(*ins, o_mla, o_dn, glu, glu, *consts)


def _ffn_kernel(h_ref, gate_ref, wg_ref, wu_ref, wd_ref, g_ref, b_ref, o_ref, acc, *, tile, tiles_per_batch, seq_rows, gated):
    c = pl.program_id(1)
    x = h_ref[...].astype(BF16)
    hid = _silu(_dot(x, wg_ref[0])) * _dot(x, wu_ref[0])
    y = _dot(hid.astype(BF16), wd_ref[0])
    if gated:
        lane = lax.broadcasted_iota(jnp.int32, (tile, LANES), 1)
        y = y * jnp.sum(jnp.where(lane == c, gate_ref[...], 0.0), axis=1, keepdims=True)

    @pl.when(c == 0)
    def _():
        acc[...] = y

    @pl.when(c > 0)
    def _():
        acc[...] = acc[...] + y

    @pl.when(c == pl.num_programs(1) - 1)
    def _():
        h2 = _layer_norm(ALPHA * h_ref[...] + acc[...], g_ref[...], b_ref[...])
        row = (pl.program_id(0) % tiles_per_batch) * tile + lax.broadcasted_iota(jnp.int32, (tile, 1), 0)
        live = (row >= FRONT_PAD) & (row < seq_rows)
        o_ref[...] = jnp.where(live, h2, 0.0)


def _ffn_call(h, gates, wg, wu, wd, ln_g, ln_b, batch, seq_rows, gated, to_frames):
    tp = h.shape[0]
    tile = ROW_TILE
    tpb = tp // batch // tile
    nchunk, _, width = wg.shape
    if to_frames:
        out_spec = _frames_spec(tile, tpb)
        out_rows = tp - batch * tile
        semantics = ("arbitrary", "arbitrary")
    else:
        out_spec = pl.BlockSpec((tile, D_MODEL), lambda i, c: (i, 0))
        out_rows = tp
        semantics = ("parallel", "arbitrary")
    return pl.pallas_call(
        functools.partial(_ffn_kernel, tile=tile, tiles_per_batch=tpb, seq_rows=seq_rows, gated=gated),
        grid=(tp // tile, nchunk),
        in_specs=[
            pl.BlockSpec((tile, D_MODEL), lambda i, c: (i, 0)),
            pl.BlockSpec((tile, LANES), lambda i, c: (i, 0)),
            pl.BlockSpec((1, D_MODEL, width), lambda i, c: (c, 0, 0)),
            pl.BlockSpec((1, D_MODEL, width), lambda i, c: (c, 0, 0)),
            pl.BlockSpec((1, width, D_MODEL), lambda i, c: (c, 0, 0)),
            _const_spec(ln_g.shape),
            _const_spec(ln_b.shape),
        ],
        out_specs=out_spec,
        out_shape=jax.ShapeDtypeStruct((out_rows, D_MODEL), F32),
        scratch_shapes=[pltpu.VMEM((tile, D_MODEL), F32)],
        compiler_params=pltpu.CompilerParams(dimension_semantics=semantics, vmem_limit_bytes=VMEM_LIMIT),
        name="ffn_ln2",
    )(h, gates, wg, wu, wd, ln_g, ln_b)


def _row(v, width=None):
    v = v.astype(F32).reshape(1, -1)
    if width is not None and v.shape[1] < width:
        v = jnp.pad(v, ((0, 0), (0, width - v.shape[1])))
    return v


def _layer_weights(l, w_in, mla_q_norm, mla_w_uq, mla_kv_norm, mla_w_ukv, dn_conv_w, dn_a_log, dn_dt_bias, dn_o_norm,
                   cv_dw_w, cv_dw_b, cv_ln_g, cv_ln_b, w_br_mla, w_br_dn, w_br_cv, w_mix_out, ln1_g, ln1_b, ln2_g, ln2_b,
                   moe_w_router):
    sizes = (MLA_Q_RANK, MLA_KV_RANK, MLA_ROPE, DN_QKV, DN_HEADS * DN_DV, DN_HEADS, DN_HEADS, CV_CH, CV_CH,
             D_MODEL, D_MODEL, D_MODEL)
    offs = np.cumsum((0,) + sizes)
    col = lambda i: w_in[l][:, offs[i]:offs[i + 1]]
    half = MLA_ROPE // 2
    w_kr = col(2)
    w_kr_pair = jnp.concatenate([-w_kr[:, half:], w_kr[:, :half]], axis=1)
    wa = jnp.concatenate([col(0), col(1), w_kr, w_kr_pair, col(5), col(6)], axis=1)
    wa = jnp.pad(wa, ((0, 0), (0, 512 - wa.shape[1])))
    wuq = mla_w_uq[l].reshape(MLA_Q_RANK, MLA_HEADS, MLA_QK)
    zq = jnp.zeros((MLA_Q_RANK, MLA_HEADS, LANES - MLA_QK), F32)
    wq_main = jnp.concatenate([wuq, zq], axis=2)
    wq_pair = jnp.concatenate([jnp.zeros((MLA_Q_RANK, MLA_HEADS, MLA_NOPE), F32), -wuq[:, :, MLA_NOPE + half:],
                               wuq[:, :, MLA_NOPE:MLA_NOPE + half], zq], axis=2)
    wq = jnp.concatenate([wq_main.reshape(MLA_Q_RANK, -1), wq_pair.reshape(MLA_Q_RANK, -1)], axis=1)
    wukv = mla_w_ukv[l].reshape(MLA_KV_RANK, MLA_HEADS, MLA_NOPE + MLA_V)
    wk_lat = jnp.concatenate([wukv[:, :, :MLA_NOPE], jnp.zeros((MLA_KV_RANK, MLA_HEADS, LANES - MLA_NOPE), F32)], axis=2)
    place = np.zeros((LANES, MLA_HEADS, LANES), np.float32)
    for r in range(MLA_ROPE):
        place[r, :, MLA_NOPE + r] = 1.0
    wk = jnp.concatenate([wk_lat.reshape(MLA_KV_RANK, -1), jnp.asarray(place).reshape(LANES, -1)], axis=0)
    wv = jnp.concatenate([wukv[:, :, MLA_NOPE:], jnp.zeros((MLA_KV_RANK, MLA_HEADS, LANES - MLA_V), F32)],
                         axis=2).reshape(MLA_KV_RANK, -1)
    vone = np.zeros((1, MLA_HEADS, LANES), np.float32)
    vone[:, :, MLA_V] = 1.0
    par = jnp.zeros((8, LANES), F32)
    par = par.at[0, MISC_A:MISC_A + DN_HEADS].set(dn_a_log[l].astype(F32))
    par = par.at[1, MISC_A:MISC_A + DN_HEADS].set(dn_dt_bias[l].astype(F32))
    if moe_w_router is None:
        wr = jnp.zeros((D_MODEL, LANES), F32)
    else:
        wr = jnp.pad(moe_w_router.astype(F32), ((0, 0), (0, LANES - N_EXPERTS)))
    wr_hi = wr.astype(BF16)
    wr_lo = jnp.concatenate([wr_hi, (wr - wr_hi.astype(F32)).astype(BF16)], axis=1)
    return dict(
        vone=jnp.asarray(vone.reshape(1, -1)), wr_hi=wr_hi, wr_lo=wr_lo,
        wa=wa.astype(BF16), wq=wq.astype(BF16), wk=wk.astype(BF16), wv=wv.astype(BF16),
        wb=col(3).astype(BF16), wc=col(4).astype(BF16), wd=jnp.concatenate([col(7), col(8)], axis=1).astype(BF16),
        qg=_row(mla_q_norm[l]), kvg=_row(mla_kv_norm[l]),
        dn_cw=jnp.pad(dn_conv_w[l].astype(F32), ((0, 8 - DN_CONV), (0, 0))), dn_par=par, dn_on=_row(dn_o_norm[l]),
        cv_w=jnp.pad(cv_dw_w[l].astype(F32), ((0, 32 - CV_WIDTH), (0, 0))), cv_b=_row(cv_dw_b[l]),
        cv_g=_row(cv_ln_g[l]), cv_be=_row(cv_ln_b[l]),
        wg=jnp.concatenate([col(9), col(10), col(11)], axis=1).astype(BF16),
        wbm=w_br_mla[l].astype(BF16), wbd=w_br_dn[l].astype(BF16), wbc=w_br_cv[l].astype(BF16),
        wmo=w_mix_out[l].astype(BF16), ln1_g=_row(ln1_g[l]), ln1_b=_row(ln1_b[l]),
        ln2_g=_row(ln2_g[l]), ln2_b=_row(ln2_b[l]),
    )


def _rope_tables(lp):
    pos = jnp.maximum(jnp.arange(lp, dtype=jnp.int32) - FRONT_PAD, 0).astype(F32)
    inv_freq = ROPE_THETA ** (-jnp.arange(0, MLA_ROPE, 2, dtype=F32) / MLA_ROPE)
    ang = pos[:, None] * inv_freq[None, :]
    cos2 = jnp.concatenate([jnp.cos(ang), jnp.cos(ang)], axis=1)
    sin2 = jnp.concatenate([jnp.sin(ang), jnp.sin(ang)], axis=1)
    scale = MLA_QK ** -0.5 * np.log2(np.e)
    zpad = jnp.zeros((lp, LANES - MLA_QK), F32)
    ct = jnp.concatenate([jnp.full((lp, MLA_NOPE), scale, F32), scale * cos2, zpad], axis=1)
    st = jnp.concatenate([jnp.zeros((lp, MLA_NOPE), F32), scale * sin2, zpad], axis=1)
    kc = jnp.concatenate([cos2, sin2, jnp.zeros((lp, LANES - 2 * MLA_ROPE), F32)], axis=1)
    return ct, st, kc


def kernel(x, meta_tokens, w_in, mla_q_norm, mla_w_uq, mla_kv_norm, mla_w_ukv, dn_conv_w, dn_a_log, dn_dt_bias, dn_o_norm, cv_dw_w, cv_dw_b, cv_ln_g, cv_ln_b, w_br_mla, w_br_dn, w_br_cv, w_mix_out, ln1_g, ln1_b, ln2_g, ln2_b, ffn_w_gate, ffn_w_up, ffn_w_down, moe_w_router, moe_w_gate, moe_w_up, moe_w_down):
    batch, seq, _ = x.shape
    assert FRONT_PAD + N_META == ROW_TILE and seq % ROW_TILE == 0 and seq >= ROW_TILE
    lp = seq_rows = ROW_TILE + seq
    tp = batch * lp
    head = jnp.concatenate([jnp.zeros((FRONT_PAD, D_MODEL), F32), meta_tokens.astype(F32)], axis=0)
    h = x.astype(F32).reshape(batch * seq, D_MODEL)
    ct, st, kc = _rope_tables(lp)
    tabs = dict(ct=jnp.tile(ct, (batch, 1)), st=jnp.tile(st, (batch, 1)), kc=jnp.tile(kc, (batch, 1)))
    for l in range(DEPTH):
        moe = l % 2 == 1
        lw = _layer_weights(l, w_in, mla_q_norm, mla_w_uq, mla_kv_norm, mla_w_ukv, dn_conv_w, dn_a_log, dn_dt_bias,
                            dn_o_norm, cv_dw_w, cv_dw_b, cv_ln_g, cv_ln_b, w_br_mla, w_br_dn, w_br_cv, w_mix_out,
                            ln1_g, ln1_b, ln2_g, ln2_b, moe_w_router[l // 2] if moe else None)
        src_head = head if l == 0 else None
        q, k, v, dn_pre, z, glu, misc = _proj_call(h, src_head, lw, tabs, tp, batch)
        o_mla = _attn_call(q, k, v, batch)
        o_dn = _dn_call(dn_pre, z, misc, lw, batch)
        h1, gates = _merge_call(h, src_head, o_mla, o_dn, glu, lw, batch, seq_rows, moe)
        if moe:
            wg, wu, wd = moe_w_gate[l // 2], moe_w_up[l // 2], moe_w_down[l // 2]
        else:
            nck = D_FF // D_FF_EXPERT
            wg = ffn_w_gate[l // 2].reshape(D_MODEL, nck, D_FF_EXPERT).transpose(1, 0, 2)
            wu = ffn_w_up[l // 2].reshape(D_MODEL, nck, D_FF_EXPERT).transpose(1, 0, 2)
            wd = ffn_w_down[l // 2].reshape(nck, D_FF_EXPERT, D_MODEL)
        h = _ffn_call(h1, gates, wg.astype(BF16), wu.astype(BF16), wd.astype(BF16), lw['ln2_g'], lw['ln2_b'],
                      batch, seq_rows, moe, l == DEPTH - 1)
    return h.reshape(batch, seq, D_MODEL).astype(x.dtype)
```
